```python
import jax, jax.numpy as jnp
from jax import lax

D_MODEL = 2048
BATCH = 2
SEQ = 4096
DEPTH = 2
DEC_BATCH = 32
DEC_SEQ = 16
PAST_LEN = 4096

CHUNK = 64
Q_BLOCK = 2 * CHUNK
N_A = DEPTH // 2
N_B = DEPTH - N_A
D_RNN = 2 * D_MODEL
N_GATE_BLOCKS = 16
GATE_BLOCK = D_RNN // N_GATE_BLOCKS
CONV_W = 4
LRU_C = 8.0
N_HEADS = 16
HEAD_DIM = D_MODEL // N_HEADS
D_ATT = N_HEADS * HEAD_DIM
EPS = 1e-6

kernel_name = 'hawk_stickbreak_yoco_stream_step'


def rms_norm(x, g):
    xf = x.astype(jnp.float32)
    y = xf * lax.rsqrt(jnp.mean(xf * xf, axis=-1, keepdims=True) + EPS)
    return (y * g.astype(jnp.float32)).astype(x.dtype)


def ada_mod(c, w, b):
    m = jax.nn.silu(c) @ w + b
    shift, scale, gate = jnp.split(m, 3, axis=-1)
    return shift[:, None, :], scale[:, None, :], gate[:, None, :]


def causal_conv(xb, buf, w, b):
    T = xb.shape[1]
    xp = jnp.concatenate([buf.astype(xb.dtype), xb], axis=1)
    y = b + sum(xp[:, i:i + T] * w[i] for i in range(CONV_W))
    return y, xp[:, -(CONV_W - 1):]


def block_diag(x, w, b):
    xb = x.reshape(x.shape[:-1] + (N_GATE_BLOCKS, GATE_BLOCK))
    return jnp.einsum('btnk,nkj->btnj', xb, w).reshape(x.shape) + b


def rg_lru(x, h0, w_r, b_r, w_i, b_i, lam):
    xf = x.astype(jnp.float32)
    r = jax.nn.sigmoid(block_diag(x, w_r, b_r).astype(jnp.float32))
    i = jax.nn.sigmoid(block_diag(x, w_i, b_i).astype(jnp.float32))
    log_a = -LRU_C * r * jax.nn.softplus(-lam.astype(jnp.float32))
    a = jnp.exp(log_a)
    u = jnp.sqrt(-jnp.expm1(2.0 * log_a)) * (i * xf)

    def step(h, au):
        a_t, u_t = au
        h = a_t * h + u_t
        return h, h

    hT, hs = lax.scan(step, h0.astype(jnp.float32), (jnp.swapaxes(a, 0, 1), jnp.swapaxes(u, 0, 1)))
    return jnp.swapaxes(hs, 0, 1).astype(x.dtype), hT


def recurrent_mixer(hn, h0, conv0, w_in, conv_w, conv_b, w_r, b_r, w_i, b_i, lam, w_out):
    xb, zg = jnp.split(hn @ w_in, 2, axis=-1)
    xb, conv_new = causal_conv(xb, conv0, conv_w, conv_b)
    y, hT = rg_lru(xb, h0, w_r, b_r, w_i, b_i, lam)
    return (y * jax.nn.silu(zg)) @ w_out, hT, conv_new


def sb_block(q, q_pos, k, v):
    z = jnp.einsum('bqhd,bshd->bhqs', q, k).astype(jnp.float32) * (HEAD_DIM ** -0.5)
    mask = jnp.arange(k.shape[1])[None, :] < q_pos[:, None]
    log_keep = jnp.where(mask, jax.nn.log_sigmoid(-z), 0.0)
    log_after = lax.cumsum(log_keep, axis=3, reverse=True) - log_keep
    attn = jnp.where(mask, jnp.exp(jax.nn.log_sigmoid(z) + log_after), 0.0)
    o = jnp.einsum('bhqs,bshd->bqhd', attn, v.astype(jnp.float32))
    return o.astype(q.dtype)


def stick_breaking(q, k, v):
    B, T = q.shape[0], q.shape[1]
    q_pos = (k.shape[1] - T) + jnp.arange(T)
    if T <= Q_BLOCK:
        return sb_block(q, q_pos, k, v)
    nb = T // Q_BLOCK
    qb = q.reshape(B, nb, Q_BLOCK, N_HEADS, HEAD_DIM).transpose(1, 0, 2, 3, 4)
    pb = q_pos.reshape(nb, Q_BLOCK)
    ob = lax.map(lambda a: sb_block(a[0], a[1], k, v), (qb, pb))
    return ob.transpose(1, 0, 2, 3, 4).reshape(B, T, N_HEADS, HEAD_DIM)


def stick_breaking_mixer(hn, k_all, v_all, w_in, w_out):
    B, T = hn.shape[0], hn.shape[1]
    q, zg = jnp.split(hn @ w_in, 2, axis=-1)
    o = stick_breaking(q.reshape(B, T, N_HEADS, HEAD_DIM), k_all, v_all)
    return (o.reshape(B, T, D_ATT) * jax.nn.silu(zg)) @ w_out


def shared_kv(x, g_kv, w_kv):
    B, T = x.shape[0], x.shape[1]
    k, v = jnp.split(rms_norm(x, g_kv) @ w_kv, 2, axis=-1)
    return k.reshape(B, T, N_HEADS, HEAD_DIM), v.reshape(B, T, N_HEADS, HEAD_DIM)


def run_group(x, c, h0, conv0, past_k, past_v, g_pre, g_post, w_ada, b_ada, w_in_a, conv_w, conv_b,
              w_rgate, b_rgate, w_igate, b_igate, lru_lambda, w_out_a, g_kv, w_kv, w_in_b, w_out_b):
    hs, convs = [], []
    k_new = v_new = k_all = v_all = None
    for l in range(DEPTH):
        shift, scale, gate = ada_mod(c, w_ada[l], b_ada[l])
        hn = rms_norm(x, g_pre[l]) * (1.0 + scale) + shift
        if l < N_A:
            out, hT, cb = recurrent_mixer(hn, h0[l], conv0[l], w_in_a[l], conv_w[l], conv_b[l],
                                          w_rgate[l], b_rgate[l], w_igate[l], b_igate[l],
                                          lru_lambda[l], w_out_a[l])
            hs.append(hT)
            convs.append(cb)
        else:
            if l == N_A:
                k_new, v_new = shared_kv(x, g_kv, w_kv)
                k_all = jnp.concatenate([past_k.astype(k_new.dtype), k_new], axis=1)
                v_all = jnp.concatenate([past_v.astype(v_new.dtype), v_new], axis=1)
            j = l - N_A
            out = stick_breaking_mixer(hn, k_all, v_all, w_in_b[j], w_out_b[j])
        x = x + gate * rms_norm(out, g_post[l])
    return x, k_new, v_new, jnp.stack(hs), jnp.stack(convs)


def setup_inputs(seed: int = 0) -> dict:
    key = jax.random.key(seed)
    ks = jax.random.split(key, 32)

    def nrm(k, shape, s):
        return jax.random.normal(k, shape, jnp.float32) * s

    u = jax.random.uniform(ks[19], (N_A, D_RNN), jnp.float32, 0.9, 0.999)
    a0 = u ** (1.0 / LRU_C)
    return {
        'x_prompt': nrm(ks[0], (BATCH, SEQ, D_MODEL), 1.0),
        'x_sample': nrm(ks[1], (DEC_BATCH, DEC_SEQ, D_MODEL), 1.0),
        'c_prompt': nrm(ks[2], (BATCH, D_MODEL), 1.0),
        'c_sample': nrm(ks[3], (DEC_BATCH, D_MODEL), 1.0),
        'cache_k': nrm(ks[4], (DEC_BATCH, PAST_LEN, N_HEADS, HEAD_DIM), 1.0),
        'cache_v': nrm(ks[5], (DEC_BATCH, PAST_LEN, N_HEADS, HEAD_DIM), 1.0),
        'state_lru': nrm(ks[6], (N_A, DEC_BATCH, D_RNN), 0.5),
        'state_conv': nrm(ks[7], (N_A, DEC_BATCH, CONV_W - 1, D_RNN), 1.0),
        'g_pre': 1.0 + nrm(ks[8], (DEPTH, D_MODEL), 0.05),
        'g_post': 1.0 + nrm(ks[9], (DEPTH, D_MODEL), 0.05),
        'w_ada': nrm(ks[10], (DEPTH, D_MODEL, 3 * D_MODEL), 0.5 * D_MODEL ** -0.5),
        'b_ada': nrm(ks[11], (DEPTH, 3 * D_MODEL), 0.01),
        'w_in_a': nrm(ks[12], (N_A, D_MODEL, 2 * D_RNN), D_MODEL ** -0.5),
        'conv_w': nrm(ks[13], (N_A, CONV_W, D_RNN), CONV_W ** -0.5),
        'conv_b': nrm(ks[14], (N_A, D_RNN), 0.01),
        'w_rgate': nrm(ks[15], (N_A, N_GATE_BLOCKS, GATE_BLOCK, GATE_BLOCK), GATE_BLOCK ** -0.5),
        'b_rgate': nrm(ks[16], (N_A, D_RNN), 0.01),
        'w_igate': nrm(ks[17], (N_A, N_GATE_BLOCKS, GATE_BLOCK, GATE_BLOCK), GATE_BLOCK ** -0.5),
        'b_igate': nrm(ks[18], (N_A, D_RNN), 0.01),
        'lru_lambda': jnp.log(a0) - jnp.log1p(-a0),
        'w_out_a': nrm(ks[20], (N_A, D_RNN, D_MODEL), D_RNN ** -0.5),
        'g_kv': 1.0 + nrm(ks[21], (D_MODEL,), 0.05),
        'w_kv': nrm(ks[22], (D_MODEL, 2 * D_ATT), D_MODEL ** -0.5),
        'w_in_b': nrm(ks[23], (N_B, D_MODEL, 2 * D_ATT), D_MODEL ** -0.5),
        'w_out_b': nrm(ks[24], (N_B, D_ATT, D_MODEL), D_ATT ** -0.5),
    }


def reference(x_prompt, x_sample, c_prompt, c_sample, cache_k, cache_v, state_lru, state_conv,
              g_pre, g_post, w_ada, b_ada, w_in_a, conv_w, conv_b, w_rgate, b_rgate, w_igate, b_igate,
              lru_lambda, w_out_a, g_kv, w_kv, w_in_b, w_out_b):
    B = x_prompt.shape[0]
    h0_p = jnp.zeros((N_A, B, D_RNN), jnp.float32)
    conv0_p = jnp.zeros((N_A, B, CONV_W - 1, D_RNN), x_prompt.dtype)
    past0 = jnp.zeros((B, 0, N_HEADS, HEAD_DIM), x_prompt.dtype)
    y_prompt, k_prompt, v_prompt, lru_prompt, conv_prompt = run_group(
        x_prompt, c_prompt, h0_p, conv0_p, past0, past0, g_pre, g_post, w_ada, b_ada, w_in_a,
        conv_w, conv_b, w_rgate, b_rgate, w_igate, b_igate, lru_lambda, w_out_a, g_kv, w_kv,
        w_in_b, w_out_b)
    y_sample, k_sample, v_sample, lru_sample, conv_sample = run_group(
        x_sample, c_sample, state_lru, state_conv, cache_k, cache_v, g_pre, g_post, w_ada, b_ada,
        w_in_a, conv_w, conv_b, w_rgate, b_rgate, w_igate, b_igate, lru_lambda, w_out_a, g_kv,
        w_kv, w_in_b, w_out_b)
    return (y_prompt, y_sample, k_prompt, v_prompt, k_sample, v_sample,
            lru_prompt, lru_sample, conv_prompt, conv_sample)
```

```python
import functools

import jax
import jax.numpy as jnp
from jax import lax
from jax.experimental import pallas as pl
from jax.experimental.pallas import tpu as pltpu

D_MODEL = 2048
DEPTH = 2
N_A = DEPTH // 2
N_B = DEPTH - N_A
D_RNN = 2 * D_MODEL
N_GATE_BLOCKS = 16
GATE_BLOCK = D_RNN // N_GATE_BLOCKS
CONV_W = 4
LRU_C = 8.0
N_HEADS = 16
HEAD_DIM = D_MODEL // N_HEADS
D_ATT = N_HEADS * HEAD_DIM
EPS = 1e-6
SCALE = HEAD_DIM ** -0.5

SUBLANES = 8
VMEM_LIMIT = 56 * 1024 * 1024

F32 = jnp.float32
BF16 = jnp.bfloat16


def _cparams(sem):
    return pltpu.CompilerParams(dimension_semantics=sem, vmem_limit_bytes=VMEM_LIMIT)


def _dot(a, b):
    return jnp.dot(a, b, preferred_element_type=F32)


def _silu(x):
    return x * jax.nn.sigmoid(x)


def _softplus(x):
    return jnp.maximum(x, 0.0) + jnp.log1p(jnp.exp(-jnp.abs(x)))


def _ada_kernel(c_ref, w_ref, b_ref, o_ref):
    c = c_ref[...]
    s = _silu(c).astype(BF16)
    o_ref[...] = _dot(s, w_ref[...].astype(BF16)) + b_ref[...]


def _ada_call(c_all, w_ada, b_ada):
    rows = c_all.shape[0]
    tn = 768
    nout = 3 * D_MODEL
    return pl.pallas_call(
        _ada_kernel,
        grid=(DEPTH, nout // tn),
        in_specs=[
            pl.BlockSpec((rows, D_MODEL), lambda l, j: (0, 0)),
            pl.BlockSpec((None, D_MODEL, tn), lambda l, j: (l, 0, j)),
            pl.BlockSpec((None, 1, tn), lambda l, j: (l, 0, j)),
        ],
        out_specs=pl.BlockSpec((None, rows, tn), lambda l, j: (l, 0, j)),
        out_shape=jax.ShapeDtypeStruct((DEPTH, rows, nout), F32),
        compiler_params=_cparams(("parallel", "parallel")),
        name="ada_mod",
    )(c_all, w_ada, b_ada.reshape(DEPTH, 1, nout))


def _nmm_kernel(x_ref, g_ref, sc_ref, sh_ref, wl_ref, wr_ref, ol_ref, or_ref, hn_ref):
    @pl.when(pl.program_id(1) == 0)
    def _():
        x = x_ref[...]
        y = x * lax.rsqrt(jnp.mean(x * x, axis=-1, keepdims=True) + EPS)
        y = y * g_ref[...]
        hn = y * (1.0 + sc_ref[...]) + sh_ref[...]
        hn_ref[...] = hn.reshape(hn_ref.shape).astype(BF16)

    hn = hn_ref[...]
    ol_ref[...] = _dot(hn, wl_ref[...]).astype(ol_ref.dtype)
    or_ref[...] = _dot(hn, wr_ref[...]).astype(or_ref.dtype)


def _nmm_call(x, g, scale, shift, w, *, bb, tt, tn, name):
    B, T, D = x.shape
    nh = w.shape[1] // 2
    nt = T // tt
    nj = nh // tn
    tm = bb * tt
    out = jax.ShapeDtypeStruct((B * T, nh), F32)
    return pl.pallas_call(
        _nmm_kernel,
        grid=((B // bb) * nt, nj),
        in_specs=[
            pl.BlockSpec((bb, tt, D), lambda i, j: (i // nt, i % nt, 0)),
            pl.BlockSpec((1, 1, D), lambda i, j: (0, 0, 0)),
            pl.BlockSpec((bb, 1, D), lambda i, j: (i // nt, 0, 0)),
            pl.BlockSpec((bb, 1, D), lambda i, j: (i // nt, 0, 0)),
            pl.BlockSpec((D, tn), lambda i, j: (0, j)),
            pl.BlockSpec((D, tn), lambda i, j: (0, j + nj)),
        ],
        out_specs=[
            pl.BlockSpec((tm, tn), lambda i, j: (i, j)),
            pl.BlockSpec((tm, tn), lambda i, j: (i, j)),
        ],
        out_shape=[out, out],
        scratch_shapes=[pltpu.VMEM((tm, D), BF16)],
        compiler_params=_cparams(("parallel", "arbitrary")),
        name=name,
    )(x, g.reshape(1, 1, D), scale, shift, w, w)


def _lru_kernel(xb_ref, zg_ref, conv0_ref, h0_ref, cw_ref, cbias_ref, wr_ref, br_ref, wi_ref, bi_ref,
                lam_ref, yz_ref, ht_ref, cn_ref, xp_ref, a_ref, u_ref, h_ref, *, bb, tt, nblk):
    pad = SUBLANES
    hist = CONV_W - 1

    @pl.when(pl.program_id(2) == 0)
    def _():
        xp_ref[:, pad - hist:pad, :] = conv0_ref[...]
        h_ref[...] = h0_ref[...]

    xp_ref[:, pad:, :] = xb_ref[...]
    xc = cbias_ref[...] + sum(
        xp_ref[:, pad - hist + i:pad - hist + i + tt, :] * cw_ref[i:i + 1, :] for i in range(CONV_W))
    cn_ref[...] = xp_ref[:, pad + tt - hist:pad + tt, :]
    xp_ref[:, 0:pad, :] = xp_ref[:, tt:tt + pad, :]

    width = nblk * GATE_BLOCK
    x2 = xc.reshape(bb * tt, width)
    x16 = x2.astype(BF16)
    rg, ig = [], []
    for k in range(nblk):
        xk = x16[:, k * GATE_BLOCK:(k + 1) * GATE_BLOCK]
        rg.append(_dot(xk, wr_ref[k]))
        ig.append(_dot(xk, wi_ref[k]))
    r = jax.nn.sigmoid(jnp.concatenate(rg, axis=1) + br_ref[...])
    ig = jax.nn.sigmoid(jnp.concatenate(ig, axis=1) + bi_ref[...])
    log_a = (-LRU_C * r) * _softplus(-lam_ref[...])
    a = jnp.exp(log_a)
    u = jnp.sqrt(-jnp.tanh(log_a) * (a * a + 1.0)) * (ig * x2)

    row = lax.broadcasted_iota(jnp.int32, (bb * tt, width), 0) & (SUBLANES - 1)
    d = 1
    while d < SUBLANES:
        a_s = pltpu.roll(a, d, axis=0)
        u_s = pltpu.roll(u, d, axis=0)
        m = row >= d
        u = jnp.where(m, a * u_s + u, u)
        a = jnp.where(m, a * a_s, a)
        d *= 2
    a_ref[...] = a.reshape(bb, tt, width)
    u_ref[...] = u.reshape(bb, tt, width)

    def body(gidx, h):
        s = pl.multiple_of(gidx * SUBLANES, SUBLANES)
        hg = a_ref[:, pl.ds(s, SUBLANES), :] * h + u_ref[:, pl.ds(s, SUBLANES), :]
        u_ref[:, pl.ds(s, SUBLANES), :] = hg
        return hg[:, SUBLANES - 1:SUBLANES, :]

    h_last = lax.fori_loop(0, tt // SUBLANES, body, h_ref[...])
    h_ref[...] = h_last
    ht_ref[...] = h_last
    yz_ref[...] = (u_ref[...] * _silu(zg_ref[...])).astype(yz_ref.dtype)


def _lru_call(xb, zg, conv0, h0, conv_w, conv_b, w_r, b_r, w_i, b_i, lam, *, bb, tt, nblk):
    B, T, Dr = xb.shape
    C = nblk * GATE_BLOCK
    hist = CONV_W - 1
    big = pl.BlockSpec((bb, tt, C), lambda b, c, t: (b, t, c))
    vec = pl.BlockSpec((1, C), lambda b, c, t: (0, c))
    gate_w = pl.BlockSpec((nblk, GATE_BLOCK, GATE_BLOCK), lambda b, c, t: (c, 0, 0))
    state3 = pl.BlockSpec((bb, hist, C), lambda b, c, t: (b, 0, c))
    state1 = pl.BlockSpec((bb, 1, C), lambda b, c, t: (b, 0, c))
    return pl.pallas_call(
        functools.partial(_lru_kernel, bb=bb, tt=tt, nblk=nblk),
        grid=(B // bb, Dr // C, T // tt),
        in_specs=[big, big, state3, state1,
                  pl.BlockSpec((CONV_W, C), lambda b, c, t: (0, c)), vec,
                  gate_w, vec, gate_w, vec, vec],
        out_specs=[big, state1, state3],
        out_shape=[jax.ShapeDtypeStruct((B, T, Dr), BF16),
                   jax.ShapeDtypeStruct((B, 1, Dr), F32),
                   jax.ShapeDtypeStruct((B, hist, Dr), F32)],
        scratch_shapes=[pltpu.VMEM((bb, tt + SUBLANES, C), F32),
                        pltpu.VMEM((bb, tt, C), F32),
                        pltpu.VMEM((bb, tt, C), F32),
                        pltpu.VMEM((bb, 1, C), F32)],
        compiler_params=_cparams(("parallel", "parallel", "arbitrary")),
        name="rglru",
    )(xb, zg, conv0, h0.reshape(B, 1, Dr), conv_w, conv_b.reshape(1, Dr), w_r, b_r.reshape(1, Dr),
      w_i, b_i.reshape(1, Dr), lam.reshape(1, Dr))


def _mpr_kernel(a_ref, w_ref, x_ref, gate_ref, g_ref, o_ref, acc_ref):
    k = pl.program_id(1)

    @pl.when(k == 0)
    def _():
        acc_ref[...] = jnp.zeros_like(acc_ref)

    acc_ref[...] += _dot(a_ref[...], w_ref[...])

    @pl.when(k == pl.num_programs(1) - 1)
    def _():
        out = acc_ref[...]
        n = out * lax.rsqrt(jnp.mean(out * out, axis=-1, keepdims=True) + EPS) * g_ref[...]
        o_ref[...] = x_ref[...] + gate_ref[...] * n.reshape(o_ref.shape)


def _mpr_call(a, w, x, gate, g, *, bb, tt, tk, name):
    B, T, D = x.shape
    K = a.shape[1]
    nt = T // tt
    tm = bb * tt
    xspec = pl.BlockSpec((bb, tt, D), lambda i, k: (i // nt, i % nt, 0))
    return pl.pallas_call(
        _mpr_kernel,
        grid=((B // bb) * nt, K // tk),
        in_specs=[
            pl.BlockSpec((tm, tk), lambda i, k: (i, k)),
            pl.BlockSpec((tk, D), lambda i, k: (k, 0)),
            xspec,
            pl.BlockSpec((bb, 1, D), lambda i, k: (i // nt, 0, 0)),
            pl.BlockSpec((1, D), lambda i, k: (0, 0)),
        ],
        out_specs=xspec,
        out_shape=jax.ShapeDtypeStruct((B, T, D), F32),
        scratch_shapes=[pltpu.VMEM((tm, D), F32)],
        compiler_params=_cparams(("parallel", "arbitrary")),
        name=name,
    )(a, w, x, gate, g.reshape(1, D))


def _rev_tri(n):
    j = lax.broadcasted_iota(jnp.int32, (n, n), 0)
    s = lax.broadcasted_iota(jnp.int32, (n, n), 1)
    return jnp.where(j >= s, 1.0, 0.0).astype(BF16)


def _sb_tile(q, kb, vb, tri, carry, mask):
    z = lax.dot_general(q, kb, (((1,), (1,)), ((), ())), preferred_element_type=F32)
    lk = -_softplus(z)
    if mask is not None:
        lk = jnp.where(mask, lk, 0.0)
    hi = lk.astype(BF16)
    lo = (lk - hi.astype(F32)).astype(BF16)
    csum = _dot(hi, tri) + _dot(lo, tri)
    p = jnp.exp(z + csum + carry)
    if mask is not None:
        p = jnp.where(mask, p, 0.0)
    return _dot(p.astype(BF16), vb), csum[:, 0:1]


def _attn_self_kernel(q_ref, zg_ref, k_ref, v_ref, o_ref, kb_ref, vb_ref, acc_ref, carry_ref, *, tq):
    i = pl.program_id(2)

    @pl.when(i == 0)
    def _():
        kb_ref[...] = k_ref[0].astype(BF16)
        vb_ref[...] = v_ref[0].astype(BF16)

    q = (q_ref[0] * SCALE).astype(BF16)
    tri = _rev_tri(tq)
    row = lax.broadcasted_iota(jnp.int32, (tq, tq), 0)
    col = lax.broadcasted_iota(jnp.int32, (tq, tq), 1)

    def tile(kt, mask):
        ks = pl.multiple_of(kt * tq, tq)
        pv, tot = _sb_tile(q, kb_ref[pl.ds(ks, tq), :], vb_ref[pl.ds(ks, tq), :], tri,
                           carry_ref[...], mask)
        acc_ref[...] += pv
        carry_ref[...] += tot

    acc_ref[...] = jnp.zeros_like(acc_ref)
    carry_ref[...] = jnp.zeros_like(carry_ref)
    tile(i, col < row)

    def body(j, c):
        tile(i - 1 - j, None)
        return c

    lax.fori_loop(0, i, body, 0)
    o_ref[0] = (acc_ref[...] * _silu(zg_ref[0])).astype(o_ref.dtype)


def _attn_self_call(q, zg, k, v, *, tq):
    B, T, _ = q.shape
    qspec = pl.BlockSpec((1, tq, HEAD_DIM), lambda b, h, i: (b, i, h))
    kspec = pl.BlockSpec((1, T, HEAD_DIM), lambda b, h, i: (b, 0, h))
    return pl.pallas_call(
        functools.partial(_attn_self_kernel, tq=tq),
        grid=(B, N_HEADS, T // tq),
        in_specs=[qspec, qspec, kspec, kspec],
        out_specs=qspec,
        out_shape=jax.ShapeDtypeStruct((B, T, D_ATT), BF16),
        scratch_shapes=[pltpu.VMEM((T, HEAD_DIM), BF16), pltpu.VMEM((T, HEAD_DIM), BF16),
                        pltpu.VMEM((tq, HEAD_DIM), F32), pltpu.VMEM((tq, 1), F32)],
        compiler_params=_cparams(("parallel", "parallel", "arbitrary")),
        name="sb_attn_self",
    )(q, zg, k, v)


def _attn_past_kernel(q_ref, zg_ref, kn_ref, vn_ref, kc_ref, vc_ref, o_ref, *, T, P, tk):
    q = (q_ref[0] * SCALE).astype(BF16)
    row = lax.broadcasted_iota(jnp.int32, (T, T), 0)
    col = lax.broadcasted_iota(jnp.int32, (T, T), 1)
    acc, carry = _sb_tile(q, kn_ref[0].astype(BF16), vn_ref[0].astype(BF16), _rev_tri(T),
                          jnp.zeros((T, 1), F32), col < row)
    tri = _rev_tri(tk)

    def body(j, c):
        acc, carry = c
        ks = pl.multiple_of(P - (j + 1) * tk, tk)
        pv, tot = _sb_tile(q, kc_ref[0, pl.ds(ks, tk), :].astype(BF16),
                           vc_ref[0, pl.ds(ks, tk), :].astype(BF16), tri, carry, None)
        return acc + pv, carry + tot

    acc, carry = lax.fori_loop(0, P // tk, body, (acc, carry))
    o_ref[0] = (acc * _silu(zg_ref[0])).astype(o_ref.dtype)


def _attn_past_call(q, zg, k_new, v_new, k_past, v_past, *, tk):
    B, T, _ = q.shape
    P = k_past.shape[1]
    qspec = pl.BlockSpec((1, T, HEAD_DIM), lambda b, h: (b, 0, h))
    pspec = pl.BlockSpec((1, P, HEAD_DIM), lambda b, h: (b, 0, h))
    return pl.pallas_call(
        functools.partial(_attn_past_kernel, T=T, P=P, tk=tk),
        grid=(B, N_HEADS),
        in_specs=[qspec, qspec, qspec, qspec, pspec, pspec],
        out_specs=qspec,
        out_shape=jax.ShapeDtypeStruct((B, T, D_ATT), BF16),
        compiler_params=_cparams(("parallel", "parallel")),
        name="sb_attn_past",
    )(q, zg, k_new, v_new, k_past, v_past)


def _run_group(x, mods, h0, conv0, past_k, past_v, p, *, bb, tt):
    B, T, D = x.shape
    hs, convs = [], []
    k_new = v_new = None
    for l in range(DEPTH):
        shift, scale, gate = mods[l]
        if l < N_A:
            xb, zg = _nmm_call(x, p["g_pre"][l], scale, shift, p["w_in_a"][l],
                               bb=bb, tt=tt, tn=512, name="in_proj_a")
            lru_bb, lru_tt = (1, 512) if T >= 512 else (B, T)
            yz, ht, cn = _lru_call(xb.reshape(B, T, D_RNN), zg.reshape(B, T, D_RNN), conv0[l], h0[l],
                                   p["conv_w"][l], p["conv_b"][l], p["w_rgate"][l], p["b_rgate"][l],
                                   p["w_igate"][l], p["b_igate"][l], p["lru_lambda"][l],
                                   bb=lru_bb, tt=lru_tt, nblk=2)
            hs.append(ht.reshape(B, D_RNN))
            convs.append(cn)
            x = _mpr_call(yz.reshape(B * T, D_RNN), p["w_out_a"][l], x, gate, p["g_post"][l],
                          bb=bb, tt=min(tt, 512), tk=512, name="out_proj_a")
        else:
            if l == N_A:
                zero = jnp.zeros_like(scale)
                k_new, v_new = _nmm_call(x, p["g_kv"], zero, zero, p["w_kv"],
                                         bb=bb, tt=tt, tn=512, name="kv_proj")
            j = l - N_A
            q, zg = _nmm_call(x, p["g_pre"][l], scale, shift, p["w_in_b"][j],
                              bb=bb, tt=tt, tn=512, name="in_proj_b")
            q3, zg3 = q.reshape(B, T, D_ATT), zg.reshape(B, T, D_ATT)
            k3, v3 = k_new.reshape(B, T, D_ATT), v_new.reshape(B, T, D_ATT)
            if past_k is None:
                oz = _attn_self_call(q3, zg3, k3, v3, tq=256)
            else:
                P = past_k.shape[1]
                oz = _attn_past_call(q3, zg3, k3, v3, past_k.reshape(B, P, D_ATT),
                                     past_v.reshape(B, P, D_ATT), tk=256)
            x = _mpr_call(oz.reshape(B * T, D_ATT), p["w_out_b"][j], x, gate, p["g_post"][l],
                          bb=bb, tt=min(tt, 512), tk=512, name="out_proj_b")
    return (x, k_new.reshape(B, T, N_HEADS, HEAD_DIM), v_new.reshape(B, T, N_HEADS, HEAD_DIM),
            jnp.stack(hs), jnp.stack(convs))


def kernel(x_prompt, x_sample, c_prompt, c_sample, cache_k, cache_v, state_lru, state_conv, g_pre, g_post, w_ada, b_ada, w_in_a, conv_w, conv_b, w_rgate, b_rgate, w_igate, b_igate, lru_lambda, w_out_a, g_kv, w_kv, w_in_b, w_out_b):
    Bp, Bs = x_prompt.shape[0], x_sample.shape[0]
    p = dict(g_pre=g_pre, g_post=g_post, conv_w=conv_w, conv_b=conv_b, b_rgate=b_rgate, b_igate=b_igate,
             lru_lambda=lru_lambda, g_kv=g_kv,
             w_in_a=w_in_a.astype(BF16), w_rgate=w_rgate.astype(BF16), w_igate=w_igate.astype(BF16),
             w_out_a=w_out_a.astype(BF16), w_kv=w_kv.astype(BF16), w_in_b=w_in_b.astype(BF16),
             w_out_b=w_out_b.astype(BF16))

    rows = Bp + Bs
    rows_pad = -(-rows // SUBLANES) * SUBLANES
    c_all = jnp.concatenate([c_prompt, c_sample, jnp.zeros((rows_pad - rows, D_MODEL), F32)], axis=0)
    m = _ada_call(c_all, w_ada, b_ada)

    def mods_of(lo, hi):
        out = []
        for l in range(DEPTH):
            ml = m[l, lo:hi]
            out.append(tuple(ml[:, i * D_MODEL:(i + 1) * D_MODEL][:, None, :] for i in range(3)))
        return out

    h0_p = jnp.zeros((N_A, Bp, D_RNN), F32)
    conv0_p = jnp.zeros((N_A, Bp, CONV_W - 1, D_RNN), F32)
    y_p, k_p, v_p, lru_p, conv_p = _run_group(x_prompt, mods_of(0, Bp), h0_p, conv0_p, None, None, p,
                                              bb=1, tt=1024)
    y_s, k_s, v_s, lru_s, conv_s = _run_group(x_sample, mods_of(Bp, rows), state_lru, state_conv,
                                              cache_k, cache_v, p, bb=Bs, tt=x_sample.shape[1])
    return (y_p, y_s, k_p, v_p, k_s, v_s, lru_p, lru_s, conv_p, conv_s)
```

```python
import functools

import jax
import jax.numpy as jnp
from jax import lax
from jax.experimental import pallas as pl
from jax.experimental.pallas import tpu as pltpu

D_MODEL = 2048
DEPTH = 2
N_A = DEPTH // 2
N_B = DEPTH - N_A
D_RNN = 2 * D_MODEL
N_GATE_BLOCKS = 16
GATE_BLOCK = D_RNN // N_GATE_BLOCKS
CONV_W = 4
LRU_C = 8.0
N_HEADS = 16
HEAD_DIM = D_MODEL // N_HEADS
D_ATT = N_HEADS * HEAD_DIM
EPS = 1e-6
LOG2E = 1.4426950408889634
QSCALE = HEAD_DIM ** -0.5 * LOG2E

SUBLANES = 8
VMEM_LIMIT = 56 * 1024 * 1024

F32 = jnp.float32
BF16 = jnp.bfloat16


def _cparams(sem):
    return pltpu.CompilerParams(dimension_semantics=sem, vmem_limit_bytes=VMEM_LIMIT)


def _dot(a, b):
    return jnp.dot(a, b, preferred_element_type=F32)


def _silu(x):
    return x * jax.nn.sigmoid(x)


def _softplus(x):
    return jnp.maximum(x, 0.0) + jnp.log(1.0 + jnp.exp(-jnp.abs(x)))


def _softplus2(x2):
    return jnp.maximum(x2, 0.0) + jnp.log(1.0 + jnp.exp2(-jnp.abs(x2))) * LOG2E


def _ada_kernel(c_ref, w_ref, b_ref, o_ref):
    c = c_ref[...]
    s = _silu(c).astype(BF16)
    o_ref[...] = _dot(s, w_ref[...].astype(BF16)) + b_ref[...]


def _ada_call(c_all, w_ada, b_ada):
    rows = c_all.shape[0]
    tn = 768
    nout = 3 * D_MODEL
    return pl.pallas_call(
        _ada_kernel,
        grid=(DEPTH, nout // tn),
        in_specs=[
            pl.BlockSpec((rows, D_MODEL), lambda l, j: (0, 0)),
            pl.BlockSpec((None, D_MODEL, tn), lambda l, j: (l, 0, j)),
            pl.BlockSpec((None, 1, tn), lambda l, j: (l, 0, j)),
        ],
        out_specs=pl.BlockSpec((None, rows, tn), lambda l, j: (l, 0, j)),
        out_shape=jax.ShapeDtypeStruct((DEPTH, rows, nout), F32),
        compiler_params=_cparams(("parallel", "parallel")),
        name="ada_mod",
    )(c_all, w_ada, b_ada.reshape(DEPTH, 1, nout))


def _nmm_kernel(x_ref, g_ref, sc_ref, sh_ref, wl_ref, wr_ref, ol_ref, or_ref, hn_ref):
    @pl.when(pl.program_id(1) == 0)
    def _():
        x = x_ref[...]
        y = x * lax.rsqrt(jnp.mean(x * x, axis=-1, keepdims=True) + EPS)
        y = y * g_ref[...]
        hn = y * (1.0 + sc_ref[...]) + sh_ref[...]
        hn_ref[...] = hn.reshape(hn_ref.shape).astype(BF16)

    hn = hn_ref[...]
    ol_ref[...] = _dot(hn, wl_ref[...]).astype(ol_ref.dtype)
    or_ref[...] = _dot(hn, wr_ref[...]).astype(or_ref.dtype)


def _nmm_call(x, g, scale, shift, w, *, bb, tt, tn, name):
    B, T, D = x.shape
    nh = w.shape[1] // 2
    nt = T // tt
    nj = nh // tn
    tm = bb * tt
    out = jax.ShapeDtypeStruct((B * T, nh), F32)
    return pl.pallas_call(
        _nmm_kernel,
        grid=((B // bb) * nt, nj),
        in_specs=[
            pl.BlockSpec((bb, tt, D), lambda i, j: (i // nt, i % nt, 0)),
            pl.BlockSpec((1, 1, D), lambda i, j: (0, 0, 0)),
            pl.BlockSpec((bb, 1, D), lambda i, j: (i // nt, 0, 0)),
            pl.BlockSpec((bb, 1, D), lambda i, j: (i // nt, 0, 0)),
            pl.BlockSpec((D, tn), lambda i, j: (0, j)),
            pl.BlockSpec((D, tn), lambda i, j: (0, j + nj)),
        ],
        out_specs=[
            pl.BlockSpec((tm, tn), lambda i, j: (i, j)),
            pl.BlockSpec((tm, tn), lambda i, j: (i, j)),
        ],
        out_shape=[out, out],
        scratch_shapes=[pltpu.VMEM((tm, D), BF16)],
        compiler_params=_cparams(("parallel", "arbitrary")),
        name=name,
    )(x, g.reshape(1, 1, D), scale, shift, w, w)


def _lru_kernel(xb_ref, zg_ref, conv0_ref, h0_ref, cw_ref, cbias_ref, wr_ref, br_ref, wi_ref, bi_ref,
                lam_ref, yz_ref, ht_ref, cn_ref, xp_ref, a_ref, u_ref, h_ref, *, bb, tt, nblk):
    pad = SUBLANES
    hist = CONV_W - 1

    @pl.when(pl.program_id(2) == 0)
    def _():
        xp_ref[:, pad - hist:pad, :] = conv0_ref[...]
        h_ref[...] = h0_ref[...]

    xp_ref[:, pad:, :] = xb_ref[...]
    xc = cbias_ref[...] + sum(
        xp_ref[:, pad - hist + i:pad - hist + i + tt, :] * cw_ref[i:i + 1, :] for i in range(CONV_W))
    cn_ref[...] = xp_ref[:, pad + tt - hist:pad + tt, :]
    xp_ref[:, 0:pad, :] = xp_ref[:, tt:tt + pad, :]

    width = nblk * GATE_BLOCK
    x2 = xc.reshape(bb * tt, width)
    x16 = x2.astype(BF16)
    rg, ig = [], []
    for k in range(nblk):
        xk = x16[:, k * GATE_BLOCK:(k + 1) * GATE_BLOCK]
        rg.append(_dot(xk, wr_ref[k]))
        ig.append(_dot(xk, wi_ref[k]))
    r = jax.nn.sigmoid(jnp.concatenate(rg, axis=1) + br_ref[...])
    ig = jax.nn.sigmoid(jnp.concatenate(ig, axis=1) + bi_ref[...])
    log_a = (-LRU_C * r) * _softplus(-lam_ref[...])
    a = jnp.exp(log_a)
    u = jnp.sqrt(-jnp.tanh(log_a) * (a * a + 1.0)) * (ig * x2)

    row = lax.broadcasted_iota(jnp.int32, (bb * tt, width), 0) & (SUBLANES - 1)
    d = 1
    while d < SUBLANES:
        a_s = pltpu.roll(a, d, axis=0)
        u_s = pltpu.roll(u, d, axis=0)
        m = row >= d
        u = jnp.where(m, a * u_s + u, u)
        a = jnp.where(m, a * a_s, a)
        d *= 2
    a_ref[...] = a.reshape(bb, tt, width)
    u_ref[...] = u.reshape(bb, tt, width)

    def body(gidx, h):
        s = pl.multiple_of(gidx * SUBLANES, SUBLANES)
        hg = a_ref[:, pl.ds(s, SUBLANES), :] * h + u_ref[:, pl.ds(s, SUBLANES), :]
        u_ref[:, pl.ds(s, SUBLANES), :] = hg
        return hg[:, SUBLANES - 1:SUBLANES, :]

    h_last = lax.fori_loop(0, tt // SUBLANES, body, h_ref[...])
    h_ref[...] = h_last
    ht_ref[...] = h_last
    yz_ref[...] = (u_ref[...] * _silu(zg_ref[...])).astype(yz_ref.dtype)


def _lru_call(xb, zg, conv0, h0, conv_w, conv_b, w_r, b_r, w_i, b_i, lam, *, bb, tt, nblk):
    B, T, Dr = xb.shape
    C = nblk * GATE_BLOCK
    hist = CONV_W - 1
    big = pl.BlockSpec((bb, tt, C), lambda b, c, t: (b, t, c))
    vec = pl.BlockSpec((1, C), lambda b, c, t: (0, c))
    gate_w = pl.BlockSpec((nblk, GATE_BLOCK, GATE_BLOCK), lambda b, c, t: (c, 0, 0))
    state3 = pl.BlockSpec((bb, hist, C), lambda b, c, t: (b, 0, c))
    state1 = pl.BlockSpec((bb, 1, C), lambda b, c, t: (b, 0, c))
    return pl.pallas_call(
        functools.partial(_lru_kernel, bb=bb, tt=tt, nblk=nblk),
        grid=(B // bb, Dr // C, T // tt),
        in_specs=[big, big, state3, state1,
                  pl.BlockSpec((CONV_W, C), lambda b, c, t: (0, c)), vec,
                  gate_w, vec, gate_w, vec, vec],
        out_specs=[big, state1, state3],
        out_shape=[jax.ShapeDtypeStruct((B, T, Dr), BF16),
                   jax.ShapeDtypeStruct((B, 1, Dr), F32),
                   jax.ShapeDtypeStruct((B, hist, Dr), F32)],
        scratch_shapes=[pltpu.VMEM((bb, tt + SUBLANES, C), F32),
                        pltpu.VMEM((bb, tt, C), F32),
                        pltpu.VMEM((bb, tt, C), F32),
                        pltpu.VMEM((bb, 1, C), F32)],
        compiler_params=_cparams(("parallel", "parallel", "arbitrary")),
        name="rglru",
    )(xb, zg, conv0, h0.reshape(B, 1, Dr), conv_w, conv_b.reshape(1, Dr), w_r, b_r.reshape(1, Dr),
      w_i, b_i.reshape(1, Dr), lam.reshape(1, Dr))


def _mpr_kernel(a_ref, w_ref, x_ref, gate_ref, g_ref, o_ref, acc_ref):
    k = pl.program_id(1)

    @pl.when(k == 0)
    def _():
        acc_ref[...] = jnp.zeros_like(acc_ref)

    acc_ref[...] += _dot(a_ref[...], w_ref[...])

    @pl.when(k == pl.num_programs(1) - 1)
    def _():
        out = acc_ref[...]
        n = out * lax.rsqrt(jnp.mean(out * out, axis=-1, keepdims=True) + EPS) * g_ref[...]
        o_ref[...] = x_ref[...] + gate_ref[...] * n.reshape(o_ref.shape)


def _mpr_call(a, w, x, gate, g, *, bb, tt, tk, name):
    B, T, D = x.shape
    K = a.shape[1]
    nt = T // tt
    tm = bb * tt
    xspec = pl.BlockSpec((bb, tt, D), lambda i, k: (i // nt, i % nt, 0))
    return pl.pallas_call(
        _mpr_kernel,
        grid=((B // bb) * nt, K // tk),
        in_specs=[
            pl.BlockSpec((tm, tk), lambda i, k: (i, k)),
            pl.BlockSpec((tk, D), lambda i, k: (k, 0)),
            xspec,
            pl.BlockSpec((bb, 1, D), lambda i, k: (i // nt, 0, 0)),
            pl.BlockSpec((1, D), lambda i, k: (0, 0)),
        ],
        out_specs=xspec,
        out_shape=jax.ShapeDtypeStruct((B, T, D), F32),
        scratch_shapes=[pltpu.VMEM((tm, D), F32)],
        compiler_params=_cparams(("parallel", "arbitrary")),
        name=name,
    )(a, w, x, gate, g.reshape(1, D))


def _rev_tri(n):
    j = lax.broadcasted_iota(jnp.int32, (n, n), 0)
    s = lax.broadcasted_iota(jnp.int32, (n, n), 1)
    return jnp.where(j >= s, 1.0, 0.0).astype(BF16)


def _sb_tile(q, kb, vb, tri, carry, mask):
    z2 = lax.dot_general(q, kb, (((1,), (1,)), ((), ())), preferred_element_type=F32)
    sp = _softplus2(z2)
    if mask is not None:
        sp = jnp.where(mask, sp, 0.0)
    hi = sp.astype(BF16)
    lo = (sp - hi.astype(F32)).astype(BF16)
    csum = _dot(hi, tri) + _dot(lo, tri)
    p = jnp.exp2(z2 - csum - carry)
    if mask is not None:
        p = jnp.where(mask, p, 0.0)
    return _dot(p.astype(BF16), vb), csum[:, 0:1]


def _attn_self_kernel(q_ref, zg_ref, k_ref, v_ref, o_ref, kb_ref, vb_ref, acc_ref, carry_ref,
                      z_ref, hi_ref, lo_ref, *, tq, tk):
    i = pl.program_id(2)

    @pl.when(i == 0)
    def _():
        kb_ref[...] = k_ref[0].astype(BF16)
        vb_ref[...] = v_ref[0].astype(BF16)

    q = (q_ref[0] * QSCALE).astype(BF16)
    tri = _rev_tri(tk)
    nd = tq // tk
    n = i * nd

    def tile(kt, r0, mask):
        ks = pl.multiple_of(kt * tk, tk)
        pv, tot = _sb_tile(q[r0:], kb_ref[pl.ds(ks, tk), :], vb_ref[pl.ds(ks, tk), :], tri,
                           carry_ref[r0:, :], mask)
        acc_ref[r0:, :] += pv
        carry_ref[r0:, :] += tot

    def scores(kt, slot):
        ks = pl.multiple_of(kt * tk, tk)
        z2 = lax.dot_general(q, kb_ref[pl.ds(ks, tk), :], (((1,), (1,)), ((), ())),
                             preferred_element_type=F32)
        sp = _softplus2(z2)
        hi = sp.astype(BF16)
        z_ref[slot] = z2
        hi_ref[slot] = hi
        lo_ref[slot] = (sp - hi.astype(F32)).astype(BF16)

    def weights(kt, slot):
        ks = pl.multiple_of(kt * tk, tk)
        csum = _dot(hi_ref[slot], tri) + _dot(lo_ref[slot], tri)
        p = jnp.exp2(z_ref[slot] - csum - carry_ref[...])
        acc_ref[...] += _dot(p.astype(BF16), vb_ref[pl.ds(ks, tk), :])
        carry_ref[...] += csum[:, 0:1]

    acc_ref[...] = jnp.zeros_like(acc_ref)
    carry_ref[...] = jnp.zeros_like(carry_ref)
    scores(jnp.maximum(n - 1, 0), 0)
    for c in reversed(range(nd)):
        rows = tq - c * tk
        row = lax.broadcasted_iota(jnp.int32, (rows, tk), 0)
        col = lax.broadcasted_iota(jnp.int32, (rows, tk), 1)
        tile(n + c, c * tk, col < row)

    def body(m, c):
        t = 2 * m
        scores(n - 2 - t, 1)
        weights(n - 1 - t, 0)
        scores(jnp.maximum(n - 3 - t, 0), 0)
        weights(n - 2 - t, 1)
        return c

    lax.fori_loop(0, n // 2, body, 0)
    o_ref[0] = (acc_ref[...] * _silu(zg_ref[0])).astype(o_ref.dtype)


def _attn_self_call(q, zg, k, v, *, tq, tk):
    B, T, _ = q.shape
    assert (tq // tk) % 2 == 0 and tq % tk == 0 and T % tq == 0
    qspec = pl.BlockSpec((1, tq, HEAD_DIM), lambda b, h, i: (b, i, h))
    kspec = pl.BlockSpec((1, T, HEAD_DIM), lambda b, h, i: (b, 0, h))
    return pl.pallas_call(
        functools.partial(_attn_self_kernel, tq=tq, tk=tk),
        grid=(B, N_HEADS, T // tq),
        in_specs=[qspec, qspec, kspec, kspec],
        out_specs=qspec,
        out_shape=jax.ShapeDtypeStruct((B, T, D_ATT), BF16),
        scratch_shapes=[pltpu.VMEM((T, HEAD_DIM), BF16), pltpu.VMEM((T, HEAD_DIM), BF16),
                        pltpu.VMEM((tq, HEAD_DIM), F32), pltpu.VMEM((tq, 1), F32),
                        pltpu.VMEM((2, tq, tk), F32), pltpu.VMEM((2, tq, tk), BF16),
                        pltpu.VMEM((2, tq, tk), BF16)],
        compiler_params=_cparams(("parallel", "parallel", "arbitrary")),
        name="sb_attn_self",
    )(q, zg, k, v)


def _attn_past_kernel(q_ref, zg_ref, kn_ref, vn_ref, kc_ref, vc_ref, o_ref, acc_ref, carry_ref, *, T, tp, blk):
    j = pl.program_id(2)
    nh = SUBLANES
    nc = tp // blk
    q_all = (q_ref[0] * QSCALE).astype(BF16)

    @pl.when(j == 0)
    def _():
        row = lax.broadcasted_iota(jnp.int32, (T, T), 0)
        col = lax.broadcasted_iota(jnp.int32, (T, T), 1)
        tri_new = _rev_tri(T)
        kn = kn_ref[0].astype(BF16)
        vn = vn_ref[0].astype(BF16)
        for h in range(nh):
            sl = slice(h * HEAD_DIM, (h + 1) * HEAD_DIM)
            pv, tot = _sb_tile(q_all[:, sl], kn[:, sl], vn[:, sl], tri_new, jnp.zeros((T, 1), F32),
                               col < row)
            acc_ref[h] = pv
            carry_ref[h] = tot

    kc2 = kc_ref.reshape(tp * nh, HEAD_DIM)
    vc2 = vc_ref.reshape(tp * nh, HEAD_DIM)
    zs, lks = [], []
    for h in range(nh):
        kh = kc2[pl.ds(h, tp, stride=nh), :].astype(BF16)
        z = lax.dot_general(q_all[:, h * HEAD_DIM:(h + 1) * HEAD_DIM], kh, (((1,), (1,)), ((), ())),
                            preferred_element_type=F32)
        zs.append(z)
        lks.append(_softplus2(z))
    stack = jnp.concatenate([lk[:, c * blk:(c + 1) * blk] for lk in lks for c in range(nc)], axis=0)
    hi = stack.astype(BF16)
    lo = (stack - hi.astype(F32)).astype(BF16)
    tri = _rev_tri(blk)
    cs = _dot(hi, tri) + _dot(lo, tri)
    for h in range(nh):
        carry = carry_ref[h]
        ps = [None] * nc
        for c in reversed(range(nc)):
            r0 = (h * nc + c) * T
            csum = cs[r0:r0 + T]
            ps[c] = jnp.exp2(zs[h][:, c * blk:(c + 1) * blk] - csum - carry)
            carry = carry + csum[:, 0:1]
        p = jnp.concatenate(ps, axis=1).astype(BF16)
        vh = vc2[pl.ds(h, tp, stride=nh), :].astype(BF16)
        acc_ref[h] += _dot(p, vh)
        carry_ref[h] = carry

    @pl.when(j == pl.num_programs(2) - 1)
    def _():
        acc = jnp.concatenate([acc_ref[h] for h in range(nh)], axis=1)
        o_ref[0] = (acc * _silu(zg_ref[0])).astype(o_ref.dtype)


def _attn_past_call(q, zg, k_new, v_new, k_past, v_past, *, tp, blk):
    B, T, _ = q.shape
    P = k_past.shape[1]
    nh = SUBLANES
    ng = N_HEADS // nh
    nt = P // tp
    wide = nh * HEAD_DIM
    qspec = pl.BlockSpec((1, T, wide), lambda b, g, j: (b, 0, g))
    pspec = pl.BlockSpec((None, tp, None, nh, HEAD_DIM), lambda b, g, j: (b, nt - 1 - j, g, 0, 0))
    k5 = k_past.reshape(B, P, ng, nh, HEAD_DIM)
    v5 = v_past.reshape(B, P, ng, nh, HEAD_DIM)
    return pl.pallas_call(
        functools.partial(_attn_past_kernel, T=T, tp=tp, blk=blk),
        grid=(B, ng, nt),
        in_specs=[qspec, qspec, qspec, qspec, pspec, pspec],
        out_specs=qspec,
        out_shape=jax.ShapeDtypeStruct((B, T, D_ATT), BF16),
        scratch_shapes=[pltpu.VMEM((nh, T, HEAD_DIM), F32), pltpu.VMEM((nh, T, 1), F32)],
        compiler_params=_cparams(("parallel", "parallel", "arbitrary")),
        name="sb_attn_past",
    )(q, zg, k_new, v_new, k5, v5)


def _run_group(x, mods, h0, conv0, past_k, past_v, p, *, bb, tt):
    B, T, D = x.shape
    hs, convs = [], []
    k_new = v_new = None
    for l in range(DEPTH):
        shift, scale, gate = mods[l]
        if l < N_A:
            xb, zg = _nmm_call(x, p["g_pre"][l], scale, shift, p["w_in_a"][l],
                               bb=bb, tt=tt, tn=512, name="in_proj_a")
            lru_bb, lru_tt = (1, 512) if T >= 512 else (B, T)
            yz, ht, cn = _lru_call(xb.reshape(B, T, D_RNN), zg.reshape(B, T, D_RNN), conv0[l], h0[l],
                                   p["conv_w"][l], p["conv_b"][l], p["w_rgate"][l], p["b_rgate"][l],
                                   p["w_igate"][l], p["b_igate"][l], p["lru_lambda"][l],
                                   bb=lru_bb, tt=lru_tt, nblk=2)
            hs.append(ht.reshape(B, D_RNN))
            convs.append(cn)
            x = _mpr_call(yz.reshape(B * T, D_RNN), p["w_out_a"][l], x, gate, p["g_post"][l],
                          bb=bb, tt=min(tt, 512), tk=512, name="out_proj_a")
        else:
            if l == N_A:
                zero = jnp.zeros_like(scale)
                k_new, v_new = _nmm_call(x, p["g_kv"], zero, zero, p["w_kv"],
                                         bb=bb, tt=tt, tn=512, name="kv_proj")
            j = l - N_A
            q, zg = _nmm_call(x, p["g_pre"][l], scale, shift, p["w_in_b"][j],
                              bb=bb, tt=tt, tn=512, name="in_proj_b")
            q3, zg3 = q.reshape(B, T, D_ATT), zg.reshape(B, T, D_ATT)
            k3, v3 = k_new.reshape(B, T, D_ATT), v_new.reshape(B, T, D_ATT)
            if past_k is None:
                oz = _attn_self_call(q3, zg3, k3, v3, tq=512, tk=256)
            else:
                oz = _attn_past_call(q3, zg3, k3, v3, past_k, past_v, tp=1024, blk=256)
            x = _mpr_call(oz.reshape(B * T, D_ATT), p["w_out_b"][j], x, gate, p["g_post"][l],
                          bb=bb, tt=min(tt, 512), tk=512, name="out_proj_b")
    return (x, k_new.reshape(B, T, N_HEADS, HEAD_DIM), v_new.reshape(B, T, N_HEADS, HEAD_DIM),
            jnp.stack(hs), jnp.stack(convs))


def kernel(x_prompt, x_sample, c_prompt, c_sample, cache_k, cache_v, state_lru, state_conv, g_pre, g_post, w_ada, b_ada, w_in_a, conv_w, conv_b, w_rgate, b_rgate, w_igate, b_igate, lru_lambda, w_out_a, g_kv, w_kv, w_in_b, w_out_b):
    Bp, Bs = x_prompt.shape[0], x_sample.shape[0]
    p = dict(g_pre=g_pre, g_post=g_post, conv_w=conv_w, conv_b=conv_b, b_rgate=b_rgate, b_igate=b_igate,
             lru_lambda=lru_lambda, g_kv=g_kv,
             w_in_a=w_in_a.astype(BF16), w_rgate=w_rgate.astype(BF16), w_igate=w_igate.astype(BF16),
             w_out_a=w_out_a.astype(BF16), w_kv=w_kv.astype(BF16), w_in_b=w_in_b.astype(BF16),
             w_out_b=w_out_b.astype(BF16))

    rows = Bp + Bs
    rows_pad = -(-rows // SUBLANES) * SUBLANES
    c_all = jnp.concatenate([c_prompt, c_sample, jnp.zeros((rows_pad - rows, D_MODEL), F32)], axis=0)
    m = _ada_call(c_all, w_ada, b_ada)

    def mods_of(lo, hi):
        out = []
        for l in range(DEPTH):
            ml = m[l, lo:hi]
            out.append(tuple(ml[:, i * D_MODEL:(i + 1) * D_MODEL][:, None, :] for i in range(3)))
        return out

    h0_p = jnp.zeros((N_A, Bp, D_RNN), F32)
    conv0_p = jnp.zeros((N_A, Bp, CONV_W - 1, D_RNN), F32)
    y_p, k_p, v_p, lru_p, conv_p = _run_group(x_prompt, mods_of(0, Bp), h0_p, conv0_p, None, None, p,
                                              bb=1, tt=1024)
    y_s, k_s, v_s, lru_s, conv_s = _run_group(x_sample, mods_of(Bp, rows), state_lru, state_conv,
                                              cache_k, cache_v, p, bb=Bs, tt=x_sample.shape[1])
    return (y_p, y_s, k_p, v_p, k_s, v_s, lru_p, lru_s, conv_p, conv_s)
```

```python
import functools

import jax
import jax.numpy as jnp
from jax import lax
from jax.experimental import pallas as pl
from jax.experimental.pallas import tpu as pltpu

D_MODEL = 2048
DEPTH = 2
N_A = DEPTH // 2
N_B = DEPTH - N_A
D_RNN = 2 * D_MODEL
N_GATE_BLOCKS = 16
GATE_BLOCK = D_RNN // N_GATE_BLOCKS
CONV_W = 4
LRU_C = 8.0
N_HEADS = 16
HEAD_DIM = D_MODEL // N_HEADS
D_ATT = N_HEADS * HEAD_DIM
EPS = 1e-6
LOG2E = 1.4426950408889634
QSCALE = HEAD_DIM ** -0.5 * LOG2E

SUBLANES = 8
LANES = 128
VMEM_LIMIT = 56 * 1024 * 1024

F32 = jnp.float32
BF16 = jnp.bfloat16


def _cparams(sem):
    return pltpu.CompilerParams(dimension_semantics=sem, vmem_limit_bytes=VMEM_LIMIT)


def _dot(a, b):
    return jnp.dot(a, b, preferred_element_type=F32)


def _sigmoid(x):
    return 0.5 * jnp.tanh(0.5 * x) + 0.5


def _silu(x):
    return x * _sigmoid(x)


def _softplus(x):
    return jnp.maximum(x, 0.0) + jnp.log(1.0 + jnp.exp(-jnp.abs(x)))


def _softplus2(x2):
    return jnp.maximum(x2, 0.0) + jnp.log(1.0 + jnp.exp2(-jnp.abs(x2))) * LOG2E


def _ada_kernel(c_ref, w_ref, b_ref, o_ref):
    c = c_ref[...]
    s = _silu(c).astype(BF16)
    o_ref[...] = _dot(s, w_ref[...].astype(BF16)) + b_ref[...]


def _ada_call(c_all, w_ada, b_ada):
    rows = c_all.shape[0]
    tn = 768
    nout = 3 * D_MODEL
    return pl.pallas_call(
        _ada_kernel,
        grid=(DEPTH, nout // tn),
        in_specs=[
            pl.BlockSpec((rows, D_MODEL), lambda l, j: (0, 0)),
            pl.BlockSpec((None, D_MODEL, tn), lambda l, j: (l, 0, j)),
            pl.BlockSpec((None, 1, tn), lambda l, j: (l, 0, j)),
        ],
        out_specs=pl.BlockSpec((None, rows, tn), lambda l, j: (l, 0, j)),
        out_shape=jax.ShapeDtypeStruct((DEPTH, rows, nout), F32),
        compiler_params=_cparams(("parallel", "parallel")),
        name="ada_mod",
    )(c_all, w_ada, b_ada.reshape(DEPTH, 1, nout))


def _nmm_kernel(x_ref, g_ref, sc_ref, sh_ref, wl_ref, wr_ref, *rest):
    *out_refs, hn_ref = rest

    @pl.when(pl.program_id(1) == 0)
    def _():
        x = x_ref[...]
        y = x * lax.rsqrt(jnp.mean(x * x, axis=-1, keepdims=True) + EPS)
        y = y * g_ref[...]
        hn = y * (1.0 + sc_ref[...]) + sh_ref[...]
        hn_ref[...] = hn.reshape(hn_ref.shape).astype(BF16)

    hn = hn_ref[...]
    left = _dot(hn, wl_ref[...])
    right = _dot(hn, wr_ref[...])
    for o_ref, val in zip(out_refs, (left, right, left, right)):
        o_ref[...] = val.astype(o_ref.dtype)


def _nmm_call(x, g, scale, shift, w, *, bb, tt, tn, name, bf16_copies=False):
    B, T, D = x.shape
    nh = w.shape[1] // 2
    nt = T // tt
    nj = nh // tn
    tm = bb * tt
    out = [jax.ShapeDtypeStruct((B * T, nh), F32)] * 2
    if bf16_copies:
        out = out + [jax.ShapeDtypeStruct((B * T, nh), BF16)] * 2
    return pl.pallas_call(
        _nmm_kernel,
        grid=((B // bb) * nt, nj),
        in_specs=[
            pl.BlockSpec((bb, tt, D), lambda i, j: (i // nt, i % nt, 0)),
            pl.BlockSpec((1, 1, D), lambda i, j: (0, 0, 0)),
            pl.BlockSpec((bb, 1, D), lambda i, j: (i // nt, 0, 0)),
            pl.BlockSpec((bb, 1, D), lambda i, j: (i // nt, 0, 0)),
            pl.BlockSpec((D, tn), lambda i, j: (0, j)),
            pl.BlockSpec((D, tn), lambda i, j: (0, j + nj)),
        ],
        out_specs=[pl.BlockSpec((tm, tn), lambda i, j: (i, j))] * len(out),
        out_shape=out,
        scratch_shapes=[pltpu.VMEM((tm, D), BF16)],
        compiler_params=_cparams(("parallel", "arbitrary")),
        name=name,
    )(x, g.reshape(1, 1, D), scale, shift, w, w)


def _lru_kernel(xb_ref, zg_ref, conv0_ref, h0_ref, cw_ref, cbias_ref, wr_ref, br_ref, wi_ref, bi_ref,
                lam_ref, yz_ref, ht_ref, cn_ref, xp_ref, a_ref, u_ref, h_ref, *, bb, tt, nblk):
    pad = SUBLANES
    hist = CONV_W - 1

    @pl.when(pl.program_id(2) == 0)
    def _():
        xp_ref[:, pad - hist:pad, :] = conv0_ref[...]
        h_ref[...] = h0_ref[...]

    xp_ref[:, pad:, :] = xb_ref[...]
    xc = cbias_ref[...] + sum(
        xp_ref[:, pad - hist + i:pad - hist + i + tt, :] * cw_ref[i:i + 1, :] for i in range(CONV_W))
    cn_ref[...] = xp_ref[:, pad + tt - hist:pad + tt, :]
    xp_ref[:, 0:pad, :] = xp_ref[:, tt:tt + pad, :]

    width = nblk * GATE_BLOCK
    x2 = xc.reshape(bb * tt, width)
    x16 = x2.astype(BF16)
    rg, ig = [], []
    for k in range(nblk):
        xk = x16[:, k * GATE_BLOCK:(k + 1) * GATE_BLOCK]
        rg.append(_dot(xk, wr_ref[k]))
        ig.append(_dot(xk, wi_ref[k]))
    r = _sigmoid(jnp.concatenate(rg, axis=1) + br_ref[...])
    ig = _sigmoid(jnp.concatenate(ig, axis=1) + bi_ref[...])
    log_a = (-LRU_C * r) * _softplus(-lam_ref[...])
    a = jnp.exp(log_a)
    w = -jnp.tanh(log_a) * (a * a + 1.0)
    u = jnp.exp(0.5 * jnp.log(w)) * (ig * x2)

    nslab = width // LANES
    for s in range(nslab):
        a_ref[s] = a[:, s * LANES:(s + 1) * LANES]
        u_ref[s] = u[:, s * LANES:(s + 1) * LANES]

    ngroups = bb * tt // SUBLANES

    def step_rows(k):
        return pl.ds(k, ngroups, stride=SUBLANES)

    a_run = a_ref[:, step_rows(0), :]
    u_run = u_ref[:, step_rows(0), :]
    for k in range(1, SUBLANES):
        a_k = a_ref[:, step_rows(k), :]
        u_run = a_k * u_run + u_ref[:, step_rows(k), :]
        a_run = a_k * a_run
        a_ref[:, step_rows(k), :] = a_run
        u_ref[:, step_rows(k), :] = u_run

    h_in = jnp.stack([h_ref[:, :, s * LANES:(s + 1) * LANES] for s in range(nslab)], axis=0)
    if bb == 1:
        def body(gidx, h):
            s = pl.multiple_of(gidx * SUBLANES, SUBLANES)
            hg = a_ref[:, pl.ds(s, SUBLANES), :] * h + u_ref[:, pl.ds(s, SUBLANES), :]
            u_ref[:, pl.ds(s, SUBLANES), :] = hg
            return hg[:, SUBLANES - 1:SUBLANES, :]

        h_out = lax.fori_loop(0, tt // SUBLANES, body, h_in.reshape(nslab, 1, LANES))
        h_out = h_out.reshape(nslab, 1, 1, LANES)
        hs = u_ref[...]
    else:
        a4 = a_ref[...].reshape(nslab, bb, tt, LANES)
        u4 = u_ref[...].reshape(nslab, bb, tt, LANES)
        h_out, parts = h_in, []
        for gidx in range(tt // SUBLANES):
            sl = slice(gidx * SUBLANES, (gidx + 1) * SUBLANES)
            hg = a4[:, :, sl] * h_out + u4[:, :, sl]
            parts.append(hg)
            h_out = hg[:, :, SUBLANES - 1:SUBLANES]
        hs = jnp.concatenate(parts, axis=2).reshape(nslab, bb * tt, LANES)

    h_last = jnp.concatenate([h_out[s] for s in range(nslab)], axis=-1)
    h_ref[...] = h_last
    ht_ref[...] = h_last
    y = jnp.concatenate([hs[s] for s in range(nslab)], axis=-1).reshape(bb, tt, width)
    yz_ref[...] = (y * _silu(zg_ref[...])).astype(yz_ref.dtype)


def _lru_call(xb, zg, conv0, h0, conv_w, conv_b, w_r, b_r, w_i, b_i, lam, *, bb, tt, nblk):
    B, T, Dr = xb.shape
    C = nblk * GATE_BLOCK
    hist = CONV_W - 1
    big = pl.BlockSpec((bb, tt, C), lambda b, c, t: (b, t, c))
    vec = pl.BlockSpec((1, C), lambda b, c, t: (0, c))
    gate_w = pl.BlockSpec((nblk, GATE_BLOCK, GATE_BLOCK), lambda b, c, t: (c, 0, 0))
    state3 = pl.BlockSpec((bb, hist, C), lambda b, c, t: (b, 0, c))
    state1 = pl.BlockSpec((bb, 1, C), lambda b, c, t: (b, 0, c))
    return pl.pallas_call(
        functools.partial(_lru_kernel, bb=bb, tt=tt, nblk=nblk),
        grid=(B // bb, Dr // C, T // tt),
        in_specs=[big, big, state3, state1,
                  pl.BlockSpec((CONV_W, C), lambda b, c, t: (0, c)), vec,
                  gate_w, vec, gate_w, vec, vec],
        out_specs=[big, state1, state3],
        out_shape=[jax.ShapeDtypeStruct((B, T, Dr), BF16),
                   jax.ShapeDtypeStruct((B, 1, Dr), F32),
                   jax.ShapeDtypeStruct((B, hist, Dr), F32)],
        scratch_shapes=[pltpu.VMEM((bb, tt + SUBLANES, C), F32),
                        pltpu.VMEM((C // LANES, bb * tt, LANES), F32),
                        pltpu.VMEM((C // LANES, bb * tt, LANES), F32),
                        pltpu.VMEM((bb, 1, C), F32)],
        compiler_params=_cparams(("parallel", "parallel", "arbitrary")),
        name="rglru",
    )(xb, zg, conv0, h0.reshape(B, 1, Dr), conv_w, conv_b.reshape(1, Dr), w_r, b_r.reshape(1, Dr),
      w_i, b_i.reshape(1, Dr), lam.reshape(1, Dr))


def _mpr_kernel(a_ref, w_ref, x_ref, gate_ref, g_ref, o_ref, out_ref):
    j = pl.program_id(1)
    out_ref[j] = _dot(a_ref[...], w_ref[...])

    @pl.when(j == pl.num_programs(1) - 1)
    def _():
        out = jnp.concatenate([out_ref[c] for c in range(out_ref.shape[0])], axis=1)
        n = out * lax.rsqrt(jnp.mean(out * out, axis=-1, keepdims=True) + EPS) * g_ref[...]
        o_ref[...] = x_ref[...] + gate_ref[...] * n.reshape(o_ref.shape)


def _mpr_call(a, w, x, gate, g, *, bb, tt, tn, name):
    B, T, D = x.shape
    K = a.shape[1]
    nt = T // tt
    tm = bb * tt
    xspec = pl.BlockSpec((bb, tt, D), lambda i, j: (i // nt, i % nt, 0))
    return pl.pallas_call(
        _mpr_kernel,
        grid=((B // bb) * nt, D // tn),
        in_specs=[
            pl.BlockSpec((tm, K), lambda i, j: (i, 0)),
            pl.BlockSpec((K, tn), lambda i, j: (0, j)),
            xspec,
            pl.BlockSpec((bb, 1, D), lambda i, j: (i // nt, 0, 0)),
            pl.BlockSpec((1, D), lambda i, j: (0, 0)),
        ],
        out_specs=xspec,
        out_shape=jax.ShapeDtypeStruct((B, T, D), F32),
        scratch_shapes=[pltpu.VMEM((D // tn, tm, tn), F32)],
        compiler_params=_cparams(("parallel", "arbitrary")),
        name=name,
    )(a, w, x, gate, g.reshape(1, D))


def _rev_tri(n):
    j = lax.broadcasted_iota(jnp.int32, (n, n), 0)
    s = lax.broadcasted_iota(jnp.int32, (n, n), 1)
    return jnp.where(j >= s, 1.0, 0.0).astype(BF16)


def _sb_tile(q, kb, vb, tri, carry, mask):
    z2 = lax.dot_general(q, kb, (((1,), (1,)), ((), ())), preferred_element_type=F32)
    sp = _softplus2(z2)
    if mask is not None:
        sp = jnp.where(mask, sp, 0.0)
    hi = sp.astype(BF16)
    lo = (sp - hi.astype(F32)).astype(BF16)
    csum = _dot(hi, tri) + _dot(lo, tri)
    p = jnp.exp2(z2 - csum - carry)
    if mask is not None:
        p = jnp.where(mask, p, 0.0)
    return _dot(p.astype(BF16), vb), csum[:, 0:1]


def _attn_self_kernel(q_ref, zg_ref, k_ref, v_ref, o_ref, acc_ref, carry_ref,
                      z_ref, hi_ref, lo_ref, *, tq, tk, nh):
    i = pl.program_id(2)
    q_all = (q_ref[0] * QSCALE).astype(BF16)
    qs = [q_all[:, g * HEAD_DIM:(g + 1) * HEAD_DIM] for g in range(nh)]
    tri = _rev_tri(tk)
    nd = tq // tk
    n = i * nd

    def keys(ref, g, kt):
        ks = pl.multiple_of(kt * tk, tk)
        return ref[0, pl.ds(ks, tk), g * HEAD_DIM:(g + 1) * HEAD_DIM]

    def tile(g, kt, r0, mask):
        pv, tot = _sb_tile(qs[g][r0:], keys(k_ref, g, kt), keys(v_ref, g, kt), tri,
                           carry_ref[g, r0:, :], mask)
        acc_ref[g, r0:, :] += pv
        carry_ref[g, r0:, :] += tot

    def scores(g, kt, slot):
        z2 = lax.dot_general(qs[g], keys(k_ref, g, kt), (((1,), (1,)), ((), ())),
                             preferred_element_type=F32)
        sp = _softplus2(z2)
        hi = sp.astype(BF16)
        z_ref[g, slot] = z2
        hi_ref[g, slot] = hi
        lo_ref[g, slot] = (sp - hi.astype(F32)).astype(BF16)

    def weights(g, kt, slot):
        csum = _dot(hi_ref[g, slot], tri) + _dot(lo_ref[g, slot], tri)
        p = jnp.exp2(z_ref[g, slot] - csum - carry_ref[g])
        acc_ref[g] += _dot(p.astype(BF16), keys(v_ref, g, kt))
        carry_ref[g] += csum[:, 0:1]

    acc_ref[...] = jnp.zeros_like(acc_ref)
    carry_ref[...] = jnp.zeros_like(carry_ref)
    for g in range(nh):
        scores(g, jnp.maximum(n - 1, 0), 0)
    for c in reversed(range(nd)):
        rows = tq - c * tk
        row = lax.broadcasted_iota(jnp.int32, (rows, tk), 0)
        col = lax.broadcasted_iota(jnp.int32, (rows, tk), 1)
        for g in range(nh):
            tile(g, n + c, c * tk, col < row)

    def body(m, c):
        t = 2 * m
        for g in range(nh):
            scores(g, n - 2 - t, 1)
            weights(g, n - 1 - t, 0)
        for g in range(nh):
            scores(g, jnp.maximum(n - 3 - t, 0), 0)
            weights(g, n - 2 - t, 1)
        return c

    lax.fori_loop(0, n // 2, body, 0)
    acc = jnp.concatenate([acc_ref[g] for g in range(nh)], axis=1)
    o_ref[0] = (acc * _silu(zg_ref[0])).astype(o_ref.dtype)


def _attn_self_call(q, zg, k, v, *, tq, tk, nh):
    B, T, _ = q.shape
    assert (tq // tk) % 2 == 0 and tq % tk == 0 and T % tq == 0
    wide = nh * HEAD_DIM
    qspec = pl.BlockSpec((1, tq, wide), lambda b, h, i: (b, i, h))
    kspec = pl.BlockSpec((1, T, wide), lambda b, h, i: (b, 0, h))
    return pl.pallas_call(
        functools.partial(_attn_self_kernel, tq=tq, tk=tk, nh=nh),
        grid=(B, N_HEADS // nh, T // tq),
        in_specs=[qspec, qspec, kspec, kspec],
        out_specs=qspec,
        out_shape=jax.ShapeDtypeStruct((B, T, D_ATT), BF16),
        scratch_shapes=[pltpu.VMEM((nh, tq, HEAD_DIM), F32), pltpu.VMEM((nh, tq, 1), F32),
                        pltpu.VMEM((nh, 2, tq, tk), F32), pltpu.VMEM((nh, 2, tq, tk), BF16),
                        pltpu.VMEM((nh, 2, tq, tk), BF16)],
        compiler_params=_cparams(("parallel", "parallel", "arbitrary")),
        name="sb_attn_self",
    )(q, zg, k, v)


def _attn_past_kernel(q_ref, zg_ref, kn_ref, vn_ref, kc_ref, vc_ref, o_ref, acc_ref, carry_ref, *, T, tp, blk):
    j = pl.program_id(2)
    nh = SUBLANES
    nc = tp // blk
    q_all = (q_ref[0] * QSCALE).astype(BF16)

    @pl.when(j == 0)
    def _():
        row = lax.broadcasted_iota(jnp.int32, (T, T), 0)
        col = lax.broadcasted_iota(jnp.int32, (T, T), 1)
        tri_new = _rev_tri(T)
        kn = kn_ref[0].astype(BF16)
        vn = vn_ref[0].astype(BF16)
        for h in range(nh):
            sl = slice(h * HEAD_DIM, (h + 1) * HEAD_DIM)
            pv, tot = _sb_tile(q_all[:, sl], kn[:, sl], vn[:, sl], tri_new, jnp.zeros((T, 1), F32),
                               col < row)
            acc_ref[h] = pv
            carry_ref[h] = tot

    kc2 = kc_ref.reshape(tp * nh, HEAD_DIM)
    vc2 = vc_ref.reshape(tp * nh, HEAD_DIM)
    zs, lks = [], []
    for h in range(nh):
        kh = kc2[pl.ds(h, tp, stride=nh), :].astype(BF16)
        z = lax.dot_general(q_all[:, h * HEAD_DIM:(h + 1) * HEAD_DIM], kh, (((1,), (1,)), ((), ())),
                            preferred_element_type=F32)
        zs.append(z)
        lks.append(_softplus2(z))
    stack = jnp.concatenate([lk[:, c * blk:(c + 1) * blk] for lk in lks for c in range(nc)], axis=0)
    hi = stack.astype(BF16)
    lo = (stack - hi.astype(F32)).astype(BF16)
    tri = _rev_tri(blk)
    cs = _dot(hi, tri) + _dot(lo, tri)
    for h in range(nh):
        carry = carry_ref[h]
        ps = [None] * nc
        for c in reversed(range(nc)):
            r0 = (h * nc + c) * T
            csum = cs[r0:r0 + T]
            ps[c] = jnp.exp2(zs[h][:, c * blk:(c + 1) * blk] - csum - carry)
            carry = carry + csum[:, 0:1]
        p = jnp.concatenate(ps, axis=1).astype(BF16)
        vh = vc2[pl.ds(h, tp, stride=nh), :].astype(BF16)
        acc_ref[h] += _dot(p, vh)
        carry_ref[h] = carry

    @pl.when(j == pl.num_programs(2) - 1)
    def _():
        acc = jnp.concatenate([acc_ref[h] for h in range(nh)], axis=1)
        o_ref[0] = (acc * _silu(zg_ref[0])).astype(o_ref.dtype)


def _attn_past_call(q, zg, k_new, v_new, k_past, v_past, *, tp, blk):
    B, T, _ = q.shape
    P = k_past.shape[1]
    nh = SUBLANES
    ng = N_HEADS // nh
    nt = P // tp
    wide = nh * HEAD_DIM
    qspec = pl.BlockSpec((1, T, wide), lambda b, g, j: (b, 0, g))
    pspec = pl.BlockSpec((None, tp, None, nh, HEAD_DIM), lambda b, g, j: (b, nt - 1 - j, g, 0, 0))
    k5 = k_past.reshape(B, P, ng, nh, HEAD_DIM)
    v5 = v_past.reshape(B, P, ng, nh, HEAD_DIM)
    return pl.pallas_call(
        functools.partial(_attn_past_kernel, T=T, tp=tp, blk=blk),
        grid=(B, ng, nt),
        in_specs=[qspec, qspec, qspec, qspec, pspec, pspec],
        out_specs=qspec,
        out_shape=jax.ShapeDtypeStruct((B, T, D_ATT), BF16),
        scratch_shapes=[pltpu.VMEM((nh, T, HEAD_DIM), F32), pltpu.VMEM((nh, T, 1), F32)],
        compiler_params=_cparams(("parallel", "parallel", "arbitrary")),
        name="sb_attn_past",
    )(q, zg, k_new, v_new, k5, v5)


def _run_group(x, mods, h0, conv0, past_k, past_v, p, *, bb, tt):
    B, T, D = x.shape
    hs, convs = [], []
    k_new = v_new = None
    for l in range(DEPTH):
        shift, scale, gate = mods[l]
        if l < N_A:
            xb, zg = _nmm_call(x, p["g_pre"][l], scale, shift, p["w_in_a"][l],
                               bb=bb, tt=tt, tn=512, name="in_proj_a")
            lru_bb, lru_tt = (1, 512) if T >= 512 else (B, T)
            yz, ht, cn = _lru_call(xb.reshape(B, T, D_RNN), zg.reshape(B, T, D_RNN), conv0[l], h0[l],
                                   p["conv_w"][l], p["conv_b"][l], p["w_rgate"][l], p["b_rgate"][l],
                                   p["w_igate"][l], p["b_igate"][l], p["lru_lambda"][l],
                                   bb=lru_bb, tt=lru_tt, nblk=2)
            hs.append(ht.reshape(B, D_RNN))
            convs.append(cn)
            x = _mpr_call(yz.reshape(B * T, D_RNN), p["w_out_a"][l], x, gate, p["g_post"][l],
                          bb=bb, tt=min(tt, 512), tn=512, name="out_proj_a")
        else:
            if l == N_A:
                zero = jnp.zeros_like(scale)
                k_new, v_new, k16, v16 = _nmm_call(x, p["g_kv"], zero, zero, p["w_kv"], bb=bb, tt=tt,
                                                   tn=512, name="kv_proj", bf16_copies=True)
            j = l - N_A
            q, zg = _nmm_call(x, p["g_pre"][l], scale, shift, p["w_in_b"][j],
                              bb=bb, tt=tt, tn=512, name="in_proj_b")
            q3, zg3 = q.reshape(B, T, D_ATT), zg.reshape(B, T, D_ATT)
            k3, v3 = k16.reshape(B, T, D_ATT), v16.reshape(B, T, D_ATT)
            if past_k is None:
                oz = _attn_self_call(q3, zg3, k3, v3, tq=512, tk=256, nh=4)
            else:
                oz = _attn_past_call(q3, zg3, k3, v3, past_k, past_v, tp=1024, blk=256)
            x = _mpr_call(oz.reshape(B * T, D_ATT), p["w_out_b"][j], x, gate, p["g_post"][l],
                          bb=bb, tt=min(tt, 512), tn=512, name="out_proj_b")
    return (x, k_new.reshape(B, T, N_HEADS, HEAD_DIM), v_new.reshape(B, T, N_HEADS, HEAD_DIM),
            jnp.stack(hs), jnp.stack(convs))


def kernel(x_prompt, x_sample, c_prompt, c_sample, cache_k, cache_v, state_lru, state_conv, g_pre, g_post, w_ada, b_ada, w_in_a, conv_w, conv_b, w_rgate, b_rgate, w_igate, b_igate, lru_lambda, w_out_a, g_kv, w_kv, w_in_b, w_out_b):
    Bp, Bs = x_prompt.shape[0], x_sample.shape[0]
    p = dict(g_pre=g_pre, g_post=g_post, conv_w=conv_w, conv_b=conv_b, b_rgate=b_rgate, b_igate=b_igate,
             lru_lambda=lru_lambda, g_kv=g_kv,
             w_in_a=w_in_a.astype(BF16), w_rgate=w_rgate.astype(BF16), w_igate=w_igate.astype(BF16),
             w_out_a=w_out_a.astype(BF16), w_kv=w_kv.astype(BF16), w_in_b=w_in_b.astype(BF16),
             w_out_b=w_out_b.astype(BF16))

    rows = Bp + Bs
    rows_pad = -(-rows // SUBLANES) * SUBLANES
    c_all = jnp.concatenate([c_prompt, c_sample, jnp.zeros((rows_pad - rows, D_MODEL), F32)], axis=0)
    m = _ada_call(c_all, w_ada, b_ada)

    def mods_of(lo, hi):
        out = []
        for l in range(DEPTH):
            ml = m[l, lo:hi]
            out.append(tuple(ml[:, i * D_MODEL:(i + 1) * D_MODEL][:, None, :] for i in range(3)))
        return out

    h0_p = jnp.zeros((N_A, Bp, D_RNN), F32)
    conv0_p = jnp.zeros((N_A, Bp, CONV_W - 1, D_RNN), F32)
    y_p, k_p, v_p, lru_p, conv_p = _run_group(x_prompt, mods_of(0, Bp), h0_p, conv0_p, None, None, p,
                                              bb=1, tt=1024)
    y_s, k_s, v_s, lru_s, conv_s = _run_group(x_sample, mods_of(Bp, rows), state_lru, state_conv,
                                              cache_k, cache_v, p, bb=Bs, tt=x_sample.shape[1])
    return (y_p, y_s, k_p, v_p, k_s, v_s, lru_p, lru_s, conv_p, conv_s)
```

```python
import functools

import jax
import jax.numpy as jnp
from jax import lax
from jax.experimental import pallas as pl
from jax.experimental.pallas import tpu as pltpu

D_MODEL = 2048
DEPTH = 2
N_A = DEPTH // 2
N_B = DEPTH - N_A
D_RNN = 2 * D_MODEL
N_GATE_BLOCKS = 16
GATE_BLOCK = D_RNN // N_GATE_BLOCKS
CONV_W = 4
LRU_C = 8.0
N_HEADS = 16
HEAD_DIM = D_MODEL // N_HEADS
D_ATT = N_HEADS * HEAD_DIM
EPS = 1e-6
LOG2E = 1.4426950408889634
QSCALE = HEAD_DIM ** -0.5 * LOG2E

SUBLANES = 8
LANES = 128
VMEM_LIMIT = 56 * 1024 * 1024

F32 = jnp.float32
BF16 = jnp.bfloat16


def _cparams(sem):
    return pltpu.CompilerParams(dimension_semantics=sem, vmem_limit_bytes=VMEM_LIMIT)


def _dot(a, b):
    return jnp.dot(a, b, preferred_element_type=F32)


def _sigmoid(x):
    return 0.5 * jnp.tanh(0.5 * x) + 0.5


def _silu(x):
    return x * _sigmoid(x)


def _softplus(x):
    return jnp.maximum(x, 0.0) + jnp.log(1.0 + jnp.exp(-jnp.abs(x)))


def _softplus2(x2):
    return jnp.maximum(x2, 0.0) + jnp.log(1.0 + jnp.exp2(-jnp.abs(x2))) * LOG2E


def _ada_kernel(c_ref, w_ref, b_ref, o_ref):
    c = c_ref[...]
    s = _silu(c).astype(BF16)
    o_ref[...] = _dot(s, w_ref[...].astype(BF16)) + b_ref[...]


def _ada_call(c_all, w_ada, b_ada):
    rows = c_all.shape[0]
    tn = 768
    nout = 3 * D_MODEL
    return pl.pallas_call(
        _ada_kernel,
        grid=(DEPTH, nout // tn),
        in_specs=[
            pl.BlockSpec((rows, D_MODEL), lambda l, j: (0, 0)),
            pl.BlockSpec((None, D_MODEL, tn), lambda l, j: (l, 0, j)),
            pl.BlockSpec((None, 1, tn), lambda l, j: (l, 0, j)),
        ],
        out_specs=pl.BlockSpec((None, rows, tn), lambda l, j: (l, 0, j)),
        out_shape=jax.ShapeDtypeStruct((DEPTH, rows, nout), F32),
        compiler_params=_cparams(("parallel", "parallel")),
        name="ada_mod",
    )(c_all, w_ada, b_ada.reshape(DEPTH, 1, nout))


def _nmm_kernel(x_ref, g_ref, sc_ref, sh_ref, wl_ref, wr_ref, *rest):
    *out_refs, hn_ref = rest

    @pl.when(pl.program_id(1) == 0)
    def _():
        x = x_ref[...]
        y = x * lax.rsqrt(jnp.mean(x * x, axis=-1, keepdims=True) + EPS)
        hn = y * (g_ref[...] * (1.0 + sc_ref[...])) + sh_ref[...]
        hn_ref[...] = hn.reshape(hn_ref.shape).astype(BF16)

    hn = hn_ref[...]
    left = _dot(hn, wl_ref[...])
    right = _dot(hn, wr_ref[...])
    for o_ref, val in zip(out_refs, (left, right, left, right)):
        o_ref[...] = val.astype(o_ref.dtype)


def _nmm_call(x, g, scale, shift, w, *, bb, tt, tn, name, bf16_copies=False):
    B, T, D = x.shape
    nh = w.shape[1] // 2
    nt = T // tt
    nj = nh // tn
    tm = bb * tt
    out = [jax.ShapeDtypeStruct((B * T, nh), F32)] * 2
    if bf16_copies:
        out = out + [jax.ShapeDtypeStruct((B * T, nh), BF16)] * 2
    return pl.pallas_call(
        _nmm_kernel,
        grid=((B // bb) * nt, nj),
        in_specs=[
            pl.BlockSpec((bb, tt, D), lambda i, j: (i // nt, i % nt, 0)),
            pl.BlockSpec((1, 1, D), lambda i, j: (0, 0, 0)),
            pl.BlockSpec((bb, 1, D), lambda i, j: (i // nt, 0, 0)),
            pl.BlockSpec((bb, 1, D), lambda i, j: (i // nt, 0, 0)),
            pl.BlockSpec((D, tn), lambda i, j: (0, j)),
            pl.BlockSpec((D, tn), lambda i, j: (0, j + nj)),
        ],
        out_specs=[pl.BlockSpec((tm, tn), lambda i, j: (i, j))] * len(out),
        out_shape=out,
        scratch_shapes=[pltpu.VMEM((tm, D), BF16)],
        compiler_params=_cparams(("parallel", "arbitrary")),
        name=name,
    )(x, g.reshape(1, 1, D), scale, shift, w, w)


def _lru_kernel(xb_ref, zg_ref, conv0_ref, h0_ref, cw_ref, cbias_ref, wr_ref, br_ref, wi_ref, bi_ref,
                lam_ref, yz_ref, ht_ref, cn_ref, xp_ref, a_ref, u_ref, h_ref, *, bb, tt, nblk):
    pad = SUBLANES
    hist = CONV_W - 1

    @pl.when(pl.program_id(2) == 0)
    def _():
        xp_ref[:, pad - hist:pad, :] = conv0_ref[...]
        h_ref[...] = h0_ref[...]

    xp_ref[:, pad:, :] = xb_ref[...]
    xc = cbias_ref[...] + sum(
        xp_ref[:, pad - hist + i:pad - hist + i + tt, :] * cw_ref[i:i + 1, :] for i in range(CONV_W))
    cn_ref[...] = xp_ref[:, pad + tt - hist:pad + tt, :]
    xp_ref[:, 0:pad, :] = xp_ref[:, tt:tt + pad, :]

    width = nblk * GATE_BLOCK
    x2 = xc.reshape(bb * tt, width)
    x16 = x2.astype(BF16)
    rg, ig = [], []
    for k in range(nblk):
        xk = x16[:, k * GATE_BLOCK:(k + 1) * GATE_BLOCK]
        rg.append(_dot(xk, wr_ref[k]))
        ig.append(_dot(xk, wi_ref[k]))
    r = _sigmoid(jnp.concatenate(rg, axis=1) + br_ref[...])
    ig = _sigmoid(jnp.concatenate(ig, axis=1) + bi_ref[...])
    log_a = (-LRU_C * r) * _softplus(-lam_ref[...])
    a = jnp.exp(log_a)
    w = -jnp.tanh(log_a) * (a * a + 1.0)
    u = jnp.exp(0.5 * jnp.log(w)) * (ig * x2)

    nslab = width // LANES
    for s in range(nslab):
        a_ref[s] = a[:, s * LANES:(s + 1) * LANES]
        u_ref[s] = u[:, s * LANES:(s + 1) * LANES]

    ngroups = bb * tt // SUBLANES

    def step_rows(k):
        return pl.ds(k, ngroups, stride=SUBLANES)

    a_run = a_ref[:, step_rows(0), :]
    u_run = u_ref[:, step_rows(0), :]
    for k in range(1, SUBLANES):
        a_k = a_ref[:, step_rows(k), :]
        u_run = a_k * u_run + u_ref[:, step_rows(k), :]
        a_run = a_k * a_run
        a_ref[:, step_rows(k), :] = a_run
        u_ref[:, step_rows(k), :] = u_run

    h_in = jnp.stack([h_ref[:, :, s * LANES:(s + 1) * LANES] for s in range(nslab)], axis=0)
    if bb == 1:
        def body(gidx, h):
            s = pl.multiple_of(gidx * SUBLANES, SUBLANES)
            hg = a_ref[:, pl.ds(s, SUBLANES), :] * h + u_ref[:, pl.ds(s, SUBLANES), :]
            u_ref[:, pl.ds(s, SUBLANES), :] = hg
            return hg[:, SUBLANES - 1:SUBLANES, :]

        h_out = lax.fori_loop(0, tt // SUBLANES, body, h_in.reshape(nslab, 1, LANES))
        h_out = h_out.reshape(nslab, 1, 1, LANES)
        hs = u_ref[...]
    else:
        a4 = a_ref[...].reshape(nslab, bb, tt, LANES)
        u4 = u_ref[...].reshape(nslab, bb, tt, LANES)
        h_out, parts = h_in, []
        for gidx in range(tt // SUBLANES):
            sl = slice(gidx * SUBLANES, (gidx + 1) * SUBLANES)
            hg = a4[:, :, sl] * h_out + u4[:, :, sl]
            parts.append(hg)
            h_out = hg[:, :, SUBLANES - 1:SUBLANES]
        hs = jnp.concatenate(parts, axis=2).reshape(nslab, bb * tt, LANES)

    h_last = jnp.concatenate([h_out[s] for s in range(nslab)], axis=-1)
    h_ref[...] = h_last
    ht_ref[...] = h_last
    y = jnp.concatenate([hs[s] for s in range(nslab)], axis=-1).reshape(bb, tt, width)
    yz_ref[...] = (y * _silu(zg_ref[...])).astype(yz_ref.dtype)


def _lru_call(xb, zg, conv0, h0, conv_w, conv_b, w_r, b_r, w_i, b_i, lam, *, bb, tt, nblk):
    B, T, Dr = xb.shape
    C = nblk * GATE_BLOCK
    hist = CONV_W - 1
    big = pl.BlockSpec((bb, tt, C), lambda b, c, t: (b, t, c))
    vec = pl.BlockSpec((1, C), lambda b, c, t: (0, c))
    gate_w = pl.BlockSpec((nblk, GATE_BLOCK, GATE_BLOCK), lambda b, c, t: (c, 0, 0))
    state3 = pl.BlockSpec((bb, hist, C), lambda b, c, t: (b, 0, c))
    state1 = pl.BlockSpec((bb, 1, C), lambda b, c, t: (b, 0, c))
    return pl.pallas_call(
        functools.partial(_lru_kernel, bb=bb, tt=tt, nblk=nblk),
        grid=(B // bb, Dr // C, T // tt),
        in_specs=[big, big, state3, state1,
                  pl.BlockSpec((CONV_W, C), lambda b, c, t: (0, c)), vec,
                  gate_w, vec, gate_w, vec, vec],
        out_specs=[big, state1, state3],
        out_shape=[jax.ShapeDtypeStruct((B, T, Dr), BF16),
                   jax.ShapeDtypeStruct((B, 1, Dr), F32),
                   jax.ShapeDtypeStruct((B, hist, Dr), F32)],
        scratch_shapes=[pltpu.VMEM((bb, tt + SUBLANES, C), F32),
                        pltpu.VMEM((C // LANES, bb * tt, LANES), F32),
                        pltpu.VMEM((C // LANES, bb * tt, LANES), F32),
                        pltpu.VMEM((bb, 1, C), F32)],
        compiler_params=_cparams(("parallel", "parallel", "arbitrary")),
        name="rglru",
    )(xb, zg, conv0, h0.reshape(B, 1, Dr), conv_w, conv_b.reshape(1, Dr), w_r, b_r.reshape(1, Dr),
      w_i, b_i.reshape(1, Dr), lam.reshape(1, Dr))


def _mpr_kernel(a_ref, w_ref, x_ref, gate_ref, g_ref, o_ref, out_ref, *, tn):
    a = a_ref[...]
    for c in range(out_ref.shape[0]):
        out_ref[c] = _dot(a, w_ref[:, c * tn:(c + 1) * tn])
    out = jnp.concatenate([out_ref[c] for c in range(out_ref.shape[0])], axis=1)
    n = out * lax.rsqrt(jnp.mean(out * out, axis=-1, keepdims=True) + EPS) * g_ref[...]
    o_ref[...] = x_ref[...] + gate_ref[...] * n.reshape(o_ref.shape)


def _mpr_call(a, w, x, gate, g, *, bb, tt, tn, name):
    B, T, D = x.shape
    K = a.shape[1]
    nt = T // tt
    tm = bb * tt
    xspec = pl.BlockSpec((bb, tt, D), lambda i: (i // nt, i % nt, 0))
    return pl.pallas_call(
        functools.partial(_mpr_kernel, tn=tn),
        grid=((B // bb) * nt,),
        in_specs=[
            pl.BlockSpec((tm, K), lambda i: (i, 0)),
            pl.BlockSpec((K, D), lambda i: (0, 0), pipeline_mode=pl.Buffered(1)),
            xspec,
            pl.BlockSpec((bb, 1, D), lambda i: (i // nt, 0, 0)),
            pl.BlockSpec((1, D), lambda i: (0, 0)),
        ],
        out_specs=xspec,
        out_shape=jax.ShapeDtypeStruct((B, T, D), F32),
        scratch_shapes=[pltpu.VMEM((D // tn, tm, tn), F32)],
        compiler_params=_cparams(("parallel",)),
        name=name,
    )(a, w, x, gate, g.reshape(1, D))


def _rev_tri(n):
    j = lax.broadcasted_iota(jnp.int32, (n, n), 0)
    s = lax.broadcasted_iota(jnp.int32, (n, n), 1)
    return jnp.where(j >= s, 1.0, 0.0).astype(BF16)


def _sb_tile(q, kb, vb, tri, carry, mask):
    z2 = lax.dot_general(q, kb, (((1,), (1,)), ((), ())), preferred_element_type=F32)
    sp = _softplus2(z2)
    if mask is not None:
        sp = jnp.where(mask, sp, 0.0)
    hi = sp.astype(BF16)
    lo = (sp - hi.astype(F32)).astype(BF16)
    csum = _dot(hi, tri) + _dot(lo, tri)
    p = jnp.exp2(z2 - csum - carry)
    if mask is not None:
        p = jnp.where(mask, p, 0.0)
    return _dot(p.astype(BF16), vb), csum[:, 0:1]


def _attn_self_kernel(q_ref, zg_ref, k_ref, v_ref, o_ref, acc_ref, carry_ref,
                      z_ref, hi_ref, lo_ref, *, tq, tk, nh):
    i = pl.program_id(2)
    q_all = (q_ref[0] * QSCALE).astype(BF16)
    qs = [q_all[:, g * HEAD_DIM:(g + 1) * HEAD_DIM] for g in range(nh)]
    tri = _rev_tri(tk)
    nd = tq // tk
    n = i * nd

    def keys(ref, g, kt):
        ks = pl.multiple_of(kt * tk, tk)
        return ref[0, pl.ds(ks, tk), g * HEAD_DIM:(g + 1) * HEAD_DIM]

    def tile(g, kt, r0, mask):
        pv, tot = _sb_tile(qs[g][r0:], keys(k_ref, g, kt), keys(v_ref, g, kt), tri,
                           carry_ref[g, r0:, :], mask)
        acc_ref[g, r0:, :] += pv
        carry_ref[g, r0:, :] += tot

    def scores(g, kt, slot):
        z2 = lax.dot_general(qs[g], keys(k_ref, g, kt), (((1,), (1,)), ((), ())),
                             preferred_element_type=F32)
        sp = _softplus2(z2)
        hi = sp.astype(BF16)
        z_ref[g, slot] = z2
        hi_ref[g, slot] = hi
        lo_ref[g, slot] = (sp - hi.astype(F32)).astype(BF16)

    def weights(g, kt, slot):
        csum = _dot(hi_ref[g, slot], tri) + _dot(lo_ref[g, slot], tri)
        p = jnp.exp2(z_ref[g, slot] - csum - carry_ref[g])
        acc_ref[g] += _dot(p.astype(BF16), keys(v_ref, g, kt))
        carry_ref[g] += csum[:, 0:1]

    acc_ref[...] = jnp.zeros_like(acc_ref)
    carry_ref[...] = jnp.zeros_like(carry_ref)
    for g in range(nh):
        scores(g, jnp.maximum(n - 1, 0), 0)
    for c in reversed(range(nd)):
        rows = tq - c * tk
        row = lax.broadcasted_iota(jnp.int32, (rows, tk), 0)
        col = lax.broadcasted_iota(jnp.int32, (rows, tk), 1)
        for g in range(nh):
            tile(g, n + c, c * tk, col < row)

    def body(m, c):
        t = 2 * m
        for g in range(nh):
            scores(g, n - 2 - t, 1)
            weights(g, n - 1 - t, 0)
        for g in range(nh):
            scores(g, jnp.maximum(n - 3 - t, 0), 0)
            weights(g, n - 2 - t, 1)
        return c

    lax.fori_loop(0, n // 2, body, 0)
    acc = jnp.concatenate([acc_ref[g] for g in range(nh)], axis=1)
    o_ref[0] = (acc * _silu(zg_ref[0])).astype(o_ref.dtype)


def _attn_self_call(q, zg, k, v, *, tq, tk, nh):
    B, T, _ = q.shape
    assert (tq // tk) % 2 == 0 and tq % tk == 0 and T % tq == 0
    wide = nh * HEAD_DIM
    qspec = pl.BlockSpec((1, tq, wide), lambda b, h, i: (b, i, h))
    kspec = pl.BlockSpec((1, T, wide), lambda b, h, i: (b, 0, h))
    return pl.pallas_call(
        functools.partial(_attn_self_kernel, tq=tq, tk=tk, nh=nh),
        grid=(B, N_HEADS // nh, T // tq),
        in_specs=[qspec, qspec, kspec, kspec],
        out_specs=qspec,
        out_shape=jax.ShapeDtypeStruct((B, T, D_ATT), BF16),
        scratch_shapes=[pltpu.VMEM((nh, tq, HEAD_DIM), F32), pltpu.VMEM((nh, tq, 1), F32),
                        pltpu.VMEM((nh, 2, tq, tk), F32), pltpu.VMEM((nh, 2, tq, tk), BF16),
                        pltpu.VMEM((nh, 2, tq, tk), BF16)],
        compiler_params=_cparams(("parallel", "parallel", "arbitrary")),
        name="sb_attn_self",
    )(q, zg, k, v)


def _attn_past_kernel(q_ref, zg_ref, kn_ref, vn_ref, kc_ref, vc_ref, o_ref, acc_ref, carry_ref, *, T, tp, blk):
    j = pl.program_id(2)
    nh = SUBLANES
    nc = tp // blk
    q_all = (q_ref[0] * QSCALE).astype(BF16)

    @pl.when(j == 0)
    def _():
        lane = lax.broadcasted_iota(jnp.int32, q_all.shape, 1)
        q_bd = jnp.concatenate(
            [jnp.where((lane >= h * HEAD_DIM) & (lane < (h + 1) * HEAD_DIM), q_all, jnp.zeros_like(q_all))
             for h in range(nh)], axis=0)
        t_idx = jnp.concatenate([lax.broadcasted_iota(jnp.int32, (T, T), 0)] * nh, axis=0)
        s_idx = lax.broadcasted_iota(jnp.int32, (nh * T, T), 1)
        pv, tot = _sb_tile(q_bd, kn_ref[0].astype(BF16), vn_ref[0].astype(BF16), _rev_tri(T),
                           jnp.zeros((nh * T, 1), F32), s_idx < t_idx)
        for h in range(nh):
            acc_ref[h] = pv[h * T:(h + 1) * T, h * HEAD_DIM:(h + 1) * HEAD_DIM]
            carry_ref[h] = tot[h * T:(h + 1) * T]

    kc2 = kc_ref.reshape(tp * nh, HEAD_DIM)
    vc2 = vc_ref.reshape(tp * nh, HEAD_DIM)
    zs, lks = [], []
    for h in range(nh):
        kh = kc2[pl.ds(h, tp, stride=nh), :].astype(BF16)
        z = lax.dot_general(q_all[:, h * HEAD_DIM:(h + 1) * HEAD_DIM], kh, (((1,), (1,)), ((), ())),
                            preferred_element_type=F32)
        zs.append(z)
        lks.append(_softplus2(z))
    stack = jnp.concatenate([lk[:, c * blk:(c + 1) * blk] for lk in lks for c in range(nc)], axis=0)
    hi = stack.astype(BF16)
    lo = (stack - hi.astype(F32)).astype(BF16)
    tri = _rev_tri(blk)
    cs = _dot(hi, tri) + _dot(lo, tri)
    for h in range(nh):
        carry = carry_ref[h]
        ps = [None] * nc
        for c in reversed(range(nc)):
            r0 = (h * nc + c) * T
            csum = cs[r0:r0 + T]
            ps[c] = jnp.exp2(zs[h][:, c * blk:(c + 1) * blk] - csum - carry)
            carry = carry + csum[:, 0:1]
        p = jnp.concatenate(ps, axis=1).astype(BF16)
        vh = vc2[pl.ds(h, tp, stride=nh), :].astype(BF16)
        acc_ref[h] += _dot(p, vh)
        carry_ref[h] = carry

    @pl.when(j == pl.num_programs(2) - 1)
    def _():
        acc = jnp.concatenate([acc_ref[h] for h in range(nh)], axis=1)
        o_ref[0] = (acc * _silu(zg_ref[0])).astype(o_ref.dtype)


def _attn_past_call(q, zg, k_new, v_new, k_past, v_past, *, tp, blk):
    B, T, _ = q.shape
    P = k_past.shape[1]
    nh = SUBLANES
    ng = N_HEADS // nh
    nt = P // tp
    wide = nh * HEAD_DIM
    qspec = pl.BlockSpec((1, T, wide), lambda b, g, j: (b, 0, g))
    pspec = pl.BlockSpec((None, tp, None, nh, HEAD_DIM), lambda b, g, j: (b, nt - 1 - j, g, 0, 0))
    k5 = k_past.reshape(B, P, ng, nh, HEAD_DIM)
    v5 = v_past.reshape(B, P, ng, nh, HEAD_DIM)
    return pl.pallas_call(
        functools.partial(_attn_past_kernel, T=T, tp=tp, blk=blk),
        grid=(B, ng, nt),
        in_specs=[qspec, qspec, qspec, qspec, pspec, pspec],
        out_specs=qspec,
        out_shape=jax.ShapeDtypeStruct((B, T, D_ATT), BF16),
        scratch_shapes=[pltpu.VMEM((nh, T, HEAD_DIM), F32), pltpu.VMEM((nh, T, 1), F32)],
        compiler_params=_cparams(("parallel", "parallel", "arbitrary")),
        name="sb_attn_past",
    )(q, zg, k_new, v_new, k5, v5)


def _run_group(x, mods, h0, conv0, past_k, past_v, p, *, bb, tt):
    B, T, D = x.shape
    hs, convs = [], []
    k_new = v_new = None
    for l in range(DEPTH):
        shift, scale, gate = mods[l]
        if l < N_A:
            xb, zg = _nmm_call(x, p["g_pre"][l], scale, shift, p["w_in_a"][l],
                               bb=bb, tt=tt, tn=512, name="in_proj_a")
            lru_bb, lru_tt = (1, 512) if T >= 512 else (B, T)
            yz, ht, cn = _lru_call(xb.reshape(B, T, D_RNN), zg.reshape(B, T, D_RNN), conv0[l], h0[l],
                                   p["conv_w"][l], p["conv_b"][l], p["w_rgate"][l], p["b_rgate"][l],
                                   p["w_igate"][l], p["b_igate"][l], p["lru_lambda"][l],
                                   bb=lru_bb, tt=lru_tt, nblk=2)
            hs.append(ht.reshape(B, D_RNN))
            convs.append(cn)
            x = _mpr_call(yz.reshape(B * T, D_RNN), p["w_out_a"][l], x, gate, p["g_post"][l],
                          bb=bb, tt=min(tt, 512), tn=512, name="out_proj_a")
        else:
            if l == N_A:
                zero = jnp.zeros_like(scale)
                k_new, v_new, k16, v16 = _nmm_call(x, p["g_kv"], zero, zero, p["w_kv"], bb=bb, tt=tt,
                                                   tn=512, name="kv_proj", bf16_copies=True)
            j = l - N_A
            q, zg = _nmm_call(x, p["g_pre"][l], scale, shift, p["w_in_b"][j],
                              bb=bb, tt=tt, tn=512, name="in_proj_b")
            q3, zg3 = q.reshape(B, T, D_ATT), zg.reshape(B, T, D_ATT)
            k3, v3 = k16.reshape(B, T, D_ATT), v16.reshape(B, T, D_ATT)
            if past_k is None:
                oz = _attn_self_call(q3, zg3, k3, v3, tq=512, tk=256, nh=4)
            else:
                oz = _attn_past_call(q3, zg3, k3, v3, past_k, past_v, tp=2048, blk=256)
            x = _mpr_call(oz.reshape(B * T, D_ATT), p["w_out_b"][j], x, gate, p["g_post"][l],
                          bb=bb, tt=min(tt, 512), tn=512, name="out_proj_b")
    return (x, k_new.reshape(B, T, N_HEADS, HEAD_DIM), v_new.reshape(B, T, N_HEADS, HEAD_DIM),
            jnp.stack(hs), jnp.stack(convs))


def kernel(x_prompt, x_sample, c_prompt, c_sample, cache_k, cache_v, state_lru, state_conv, g_pre, g_post, w_ada, b_ada, w_in_a, conv_w, conv_b, w_rgate, b_rgate, w_igate, b_igate, lru_lambda, w_out_a, g_kv, w_kv, w_in_b, w_out_b):
    Bp, Bs = x_prompt.shape[0], x_sample.shape[0]
    p = dict(g_pre=g_pre, g_post=g_post, conv_w=conv_w, conv_b=conv_b, b_rgate=b_rgate, b_igate=b_igate,
             lru_lambda=lru_lambda, g_kv=g_kv,
             w_in_a=w_in_a.astype(BF16), w_rgate=w_rgate.astype(BF16), w_igate=w_igate.astype(BF16),
             w_out_a=w_out_a.astype(BF16), w_kv=w_kv.astype(BF16), w_in_b=w_in_b.astype(BF16),
             w_out_b=w_out_b.astype(BF16))

    rows = Bp + Bs
    rows_pad = -(-rows // SUBLANES) * SUBLANES
    c_all = jnp.concatenate([c_prompt, c_sample, jnp.zeros((rows_pad - rows, D_MODEL), F32)], axis=0)
    m = _ada_call(c_all, w_ada, b_ada)

    def mods_of(lo, hi):
        out = []
        for l in range(DEPTH):
            ml = m[l, lo:hi]
            out.append(tuple(ml[:, i * D_MODEL:(i + 1) * D_MODEL][:, None, :] for i in range(3)))
        return out

    h0_p = jnp.zeros((N_A, Bp, D_RNN), F32)
    conv0_p = jnp.zeros((N_A, Bp, CONV_W - 1, D_RNN), F32)
    y_p, k_p, v_p, lru_p, conv_p = _run_group(x_prompt, mods_of(0, Bp), h0_p, conv0_p, None, None, p,
                                              bb=1, tt=1024)
    y_s, k_s, v_s, lru_s, conv_s = _run_group(x_sample, mods_of(Bp, rows), state_lru, state_conv,
                                              cache_k, cache_v, p, bb=Bs, tt=x_sample.shape[1])
    return (y_p, y_s, k_p, v_p, k_s, v_s, lru_p, lru_s, conv_p, conv_s)
```

```python
import functools
from typing import NamedTuple

import jax
import jax.numpy as jnp
from jax import lax
from jax.experimental import pallas as pl
from jax.experimental.pallas import tpu as pltpu

D_MODEL = 2048
DEPTH = 2
N_A = DEPTH // 2
N_B = DEPTH - N_A
D_RNN = 2 * D_MODEL
N_GATE_BLOCKS = 16
GATE_BLOCK = D_RNN // N_GATE_BLOCKS
CONV_W = 4
LRU_C = 8.0
N_HEADS = 16
HEAD_DIM = D_MODEL // N_HEADS
D_ATT = N_HEADS * HEAD_DIM
EPS = 1e-6
LOG2E = 1.4426950408889634
QSCALE = HEAD_DIM ** -0.5 * LOG2E

SUBLANES = 8
LANES = 128
MXU_DIM = 256
VMEM_LIMIT = 56 * 1024 * 1024

F32 = jnp.float32
BF16 = jnp.bfloat16


def _cparams(sem):
    return pltpu.CompilerParams(dimension_semantics=sem, vmem_limit_bytes=VMEM_LIMIT)


def _dot(a, b):
    return jnp.dot(a, b, preferred_element_type=F32)


def _sigmoid(x):
    return 0.5 * jnp.tanh(0.5 * x) + 0.5


def _silu(x):
    h = 0.5 * x
    return h + h * jnp.tanh(h)


def _softplus(x):
    return jnp.maximum(x, 0.0) + jnp.log(1.0 + jnp.exp(-jnp.abs(x)))


def _softplus2(x2):
    return jnp.maximum(x2, 0.0) + jnp.log(1.0 + jnp.exp2(-jnp.abs(x2))) * LOG2E


def _ada_kernel(c_ref, w_ref, b_ref, o_ref):
    c = c_ref[...]
    s = _silu(c).astype(BF16)
    o_ref[...] = _dot(s, w_ref[...].astype(BF16)) + b_ref[...]


def _ada_call(c_all, w_ada, b_ada):
    rows = c_all.shape[0]
    tn = 3 * MXU_DIM
    nout = 3 * D_MODEL
    return pl.pallas_call(
        _ada_kernel,
        grid=(DEPTH, nout // tn),
        in_specs=[
            pl.BlockSpec((rows, D_MODEL), lambda l, j: (0, 0)),
            pl.BlockSpec((None, D_MODEL, tn), lambda l, j: (l, 0, j)),
            pl.BlockSpec((None, 1, tn), lambda l, j: (l, 0, j)),
        ],
        out_specs=pl.BlockSpec((None, rows, tn), lambda l, j: (l, 0, j)),
        out_shape=jax.ShapeDtypeStruct((DEPTH, rows, nout), F32),
        compiler_params=_cparams(("parallel", "parallel")),
        name="ada_mod",
    )(c_all, w_ada, b_ada.reshape(DEPTH, 1, nout))


def _nmm_kernel(x_ref, g_ref, sc_ref, sh_ref, wl_ref, wr_ref, *rest):
    *out_refs, hn_ref = rest

    @pl.when(pl.program_id(1) == 0)
    def _():
        x = x_ref[...]
        y = x * lax.rsqrt(jnp.mean(x * x, axis=-1, keepdims=True) + EPS)
        hn = y * (g_ref[...] * (1.0 + sc_ref[...])) + sh_ref[...]
        hn_ref[...] = hn.reshape(hn_ref.shape).astype(BF16)

    hn = hn_ref[...]
    left = _dot(hn, wl_ref[...])
    right = _dot(hn, wr_ref[...])
    for o_ref, val in zip(out_refs, (left, right, left, right)):
        o_ref[...] = val.astype(o_ref.dtype)


def _nmm_call(x, g, scale, shift, w, *, bb, tt, tn, name, bf16_copies=False):
    B, T, D = x.shape
    nh = w.shape[1] // 2
    nt = T // tt
    nj = nh // tn
    tm = bb * tt
    out = [jax.ShapeDtypeStruct((B * T, nh), F32)] * 2
    if bf16_copies:
        out = out + [jax.ShapeDtypeStruct((B * T, nh), BF16)] * 2
    return pl.pallas_call(
        _nmm_kernel,
        grid=((B // bb) * nt, nj),
        in_specs=[
            pl.BlockSpec((bb, tt, D), lambda i, j: (i // nt, i % nt, 0)),
            pl.BlockSpec((1, 1, D), lambda i, j: (0, 0, 0)),
            pl.BlockSpec((bb, 1, D), lambda i, j: (i // nt, 0, 0)),
            pl.BlockSpec((bb, 1, D), lambda i, j: (i // nt, 0, 0)),
            pl.BlockSpec((D, tn), lambda i, j: (0, j)),
            pl.BlockSpec((D, tn), lambda i, j: (0, j + nj)),
        ],
        out_specs=[pl.BlockSpec((tm, tn), lambda i, j: (i, j))] * len(out),
        out_shape=out,
        scratch_shapes=[pltpu.VMEM((tm, D), BF16)],
        compiler_params=_cparams(("parallel", "arbitrary")),
        name=name,
    )(x, g.reshape(1, 1, D), scale, shift, w, w)


def _lru_coeffs(x2, wr_ref, br_ref, wi_ref, bi_ref, lam_ref, nblk):
    x16 = x2.astype(BF16)
    rg, ig = [], []
    for k in range(nblk):
        xk = x16[:, k * GATE_BLOCK:(k + 1) * GATE_BLOCK]
        rg.append(_dot(xk, wr_ref[k]))
        ig.append(_dot(xk, wi_ref[k]))
    ig = _sigmoid(jnp.concatenate(ig, axis=1) + bi_ref[...])
    half = (-0.5 * LRU_C) * _softplus(-lam_ref[...])
    log_a = jnp.tanh(0.5 * (jnp.concatenate(rg, axis=1) + br_ref[...])) * half + half
    a = jnp.exp(log_a)
    w = -jnp.tanh(log_a) * (a * a + 1.0)
    u = jnp.exp2((0.5 * LOG2E) * jnp.log(w)) * (ig * x2)
    return a, u


def _lru_steps_kernel(xb_ref, zg_ref, conv0_ref, h0_ref, cw_ref, cbias_ref, wr_ref, br_ref, wi_ref,
                      bi_ref, lam_ref, yz_ref, ht_ref, cn_ref, xp_ref, hs_ref, car_ref, hin_ref, h_ref,
                      *, tt, nblk):
    pad = SUBLANES
    hist = CONV_W - 1
    width = nblk * GATE_BLOCK
    nslab = width // LANES
    M = tt // SUBLANES

    def lanes(s):
        return slice(s * LANES, (s + 1) * LANES)

    @pl.when(pl.program_id(2) == 0)
    def _():
        for s in range(nslab):
            xp_ref[s, pad - hist:pad, :] = conv0_ref[0, :, lanes(s)]
        h_ref[...] = h0_ref[...]

    for s in range(nslab):
        xp_ref[s, pad:, :] = xb_ref[0, :, lanes(s)]
    step = {m: [xp_ref[s, pl.ds(pad + m, M, stride=SUBLANES), :] for s in range(nslab)]
            for m in range(-hist, SUBLANES)}
    conv = []
    for k in range(SUBLANES):
        cols = [cbias_ref[:, lanes(s)]
                + sum(step[k - hist + i][s] * cw_ref[i:i + 1, lanes(s)] for i in range(CONV_W))
                for s in range(nslab)]
        conv.append(jnp.concatenate(cols, axis=-1))
    cn_ref[0] = jnp.concatenate([xp_ref[s, pad + tt - hist:pad + tt, :] for s in range(nslab)], axis=-1)
    for s in range(nslab):
        xp_ref[s, 0:pad, :] = xp_ref[s, tt:tt + pad, :]

    a, u = _lru_coeffs(jnp.concatenate(conv, axis=0), wr_ref, br_ref, wi_ref, bi_ref, lam_ref, nblk)

    a_run, u_run = a[0:M], u[0:M]
    a_steps, u_steps = [a_run], [u_run]
    for k in range(1, SUBLANES):
        a_k = a[k * M:(k + 1) * M]
        u_run = a_k * u_run + u[k * M:(k + 1) * M]
        a_run = a_k * a_run
        a_steps.append(a_run)
        u_steps.append(u_run)

    car_ref[0] = a_run
    car_ref[1] = u_run
    hin_ref[0:1, :] = h_ref[0]

    def body(g, h):
        h = car_ref[0, pl.ds(g, 1), :] * h + car_ref[1, pl.ds(g, 1), :]
        hin_ref[pl.ds(g + 1, 1), :] = h
        return h

    h_last = lax.fori_loop(0, M, body, h_ref[0])
    h_ref[0] = h_last
    ht_ref[0] = h_last
    h_in = hin_ref[0:M, :]
    for k in range(SUBLANES):
        h_k = a_steps[k] * h_in + u_steps[k]
        for s in range(nslab):
            hs_ref[s, pl.ds(k, M, stride=SUBLANES), :] = h_k[:, lanes(s)]
    y = jnp.concatenate([hs_ref[s] for s in range(nslab)], axis=-1)
    yz_ref[0] = (y * _silu(zg_ref[0])).astype(yz_ref.dtype)


def _lru_kernel(xb_ref, zg_ref, conv0_ref, h0_ref, cw_ref, cbias_ref, wr_ref, br_ref, wi_ref, bi_ref,
                lam_ref, yz_ref, ht_ref, cn_ref, xp_ref, a_ref, u_ref, h_ref, *, bb, tt, nblk):
    pad = SUBLANES
    hist = CONV_W - 1

    @pl.when(pl.program_id(2) == 0)
    def _():
        xp_ref[:, pad - hist:pad, :] = conv0_ref[...]
        h_ref[...] = h0_ref[...]

    xp_ref[:, pad:, :] = xb_ref[...]
    xc = cbias_ref[...] + sum(
        xp_ref[:, pad - hist + i:pad - hist + i + tt, :] * cw_ref[i:i + 1, :] for i in range(CONV_W))
    cn_ref[...] = xp_ref[:, pad + tt - hist:pad + tt, :]
    xp_ref[:, 0:pad, :] = xp_ref[:, tt:tt + pad, :]

    width = nblk * GATE_BLOCK
    a, u = _lru_coeffs(xc.reshape(bb * tt, width), wr_ref, br_ref, wi_ref, bi_ref, lam_ref, nblk)

    nslab = width // LANES
    for s in range(nslab):
        a_ref[s] = a[:, s * LANES:(s + 1) * LANES]
        u_ref[s] = u[:, s * LANES:(s + 1) * LANES]

    ngroups = bb * tt // SUBLANES

    def step_rows(k):
        return pl.ds(k, ngroups, stride=SUBLANES)

    a_run = a_ref[:, step_rows(0), :]
    u_run = u_ref[:, step_rows(0), :]
    for k in range(1, SUBLANES):
        a_k = a_ref[:, step_rows(k), :]
        u_run = a_k * u_run + u_ref[:, step_rows(k), :]
        a_run = a_k * a_run
        a_ref[:, step_rows(k), :] = a_run
        u_ref[:, step_rows(k), :] = u_run

    h_in = jnp.stack([h_ref[:, :, s * LANES:(s + 1) * LANES] for s in range(nslab)], axis=0)
    if bb == 1:
        def body(gidx, h):
            s = pl.multiple_of(gidx * SUBLANES, SUBLANES)
            hg = a_ref[:, pl.ds(s, SUBLANES), :] * h + u_ref[:, pl.ds(s, SUBLANES), :]
            u_ref[:, pl.ds(s, SUBLANES), :] = hg
            return hg[:, SUBLANES - 1:SUBLANES, :]

        h_out = lax.fori_loop(0, tt // SUBLANES, body, h_in.reshape(nslab, 1, LANES))
        h_out = h_out.reshape(nslab, 1, 1, LANES)
        hs = u_ref[...]
    else:
        a4 = a_ref[...].reshape(nslab, bb, tt, LANES)
        u4 = u_ref[...].reshape(nslab, bb, tt, LANES)
        h_out, parts = h_in, []
        for gidx in range(tt // SUBLANES):
            sl = slice(gidx * SUBLANES, (gidx + 1) * SUBLANES)
            hg = a4[:, :, sl] * h_out + u4[:, :, sl]
            parts.append(hg)
            h_out = hg[:, :, SUBLANES - 1:SUBLANES]
        hs = jnp.concatenate(parts, axis=2).reshape(nslab, bb * tt, LANES)

    h_last = jnp.concatenate([h_out[s] for s in range(nslab)], axis=-1)
    h_ref[...] = h_last
    ht_ref[...] = h_last
    y = jnp.concatenate([hs[s] for s in range(nslab)], axis=-1).reshape(bb, tt, width)
    yz_ref[...] = (y * _silu(zg_ref[...])).astype(yz_ref.dtype)


def _lru_call(xb, zg, conv0, h0, conv_w, conv_b, w_r, b_r, w_i, b_i, lam, *, bb, tt, nblk):
    B, T, Dr = xb.shape
    C = nblk * GATE_BLOCK
    hist = CONV_W - 1
    big = pl.BlockSpec((bb, tt, C), lambda b, c, t: (b, t, c))
    vec = pl.BlockSpec((1, C), lambda b, c, t: (0, c))
    gate_w = pl.BlockSpec((nblk, GATE_BLOCK, GATE_BLOCK), lambda b, c, t: (c, 0, 0))
    state3 = pl.BlockSpec((bb, hist, C), lambda b, c, t: (b, 0, c))
    state1 = pl.BlockSpec((bb, 1, C), lambda b, c, t: (b, 0, c))
    nslab = C // LANES
    if bb == 1:
        body = functools.partial(_lru_steps_kernel, tt=tt, nblk=nblk)
        scratch = [pltpu.VMEM((nslab, tt + SUBLANES, LANES), F32),
                   pltpu.VMEM((nslab, tt, LANES), F32),
                   pltpu.VMEM((2, tt // SUBLANES, C), F32),
                   pltpu.VMEM((tt // SUBLANES + SUBLANES, C), F32),
                   pltpu.VMEM((1, 1, C), F32)]
    else:
        body = functools.partial(_lru_kernel, bb=bb, tt=tt, nblk=nblk)
        scratch = [pltpu.VMEM((bb, tt + SUBLANES, C), F32),
                   pltpu.VMEM((nslab, bb * tt, LANES), F32),
                   pltpu.VMEM((nslab, bb * tt, LANES), F32),
                   pltpu.VMEM((bb, 1, C), F32)]
    return pl.pallas_call(
        body,
        grid=(B // bb, Dr // C, T // tt),
        in_specs=[big, big, state3, state1,
                  pl.BlockSpec((CONV_W, C), lambda b, c, t: (0, c)), vec,
                  gate_w, vec, gate_w, vec, vec],
        out_specs=[big, state1, state3],
        out_shape=[jax.ShapeDtypeStruct((B, T, Dr), BF16),
                   jax.ShapeDtypeStruct((B, 1, Dr), F32),
                   jax.ShapeDtypeStruct((B, hist, Dr), F32)],
        scratch_shapes=scratch,
        compiler_params=_cparams(("parallel", "parallel", "arbitrary")),
        name="rglru",
    )(xb, zg, conv0, h0.reshape(B, 1, Dr), conv_w, conv_b.reshape(1, Dr), w_r, b_r.reshape(1, Dr),
      w_i, b_i.reshape(1, Dr), lam.reshape(1, Dr))


def _mpr_kernel(a_ref, w_ref, x_ref, gate_ref, g_ref, o_ref, out_ref, *, tn):
    a = a_ref[...]
    for c in range(out_ref.shape[0]):
        out_ref[c] = _dot(a, w_ref[:, c * tn:(c + 1) * tn])
    out = jnp.concatenate([out_ref[c] for c in range(out_ref.shape[0])], axis=1)
    n = out * lax.rsqrt(jnp.mean(out * out, axis=-1, keepdims=True) + EPS) * g_ref[...]
    o_ref[...] = x_ref[...] + gate_ref[...] * n.reshape(o_ref.shape)


def _mpr_call(a, w, x, gate, g, *, bb, tt, tn, name):
    B, T, D = x.shape
    K = a.shape[1]
    nt = T // tt
    tm = bb * tt
    xspec = pl.BlockSpec((bb, tt, D), lambda i: (i // nt, i % nt, 0))
    return pl.pallas_call(
        functools.partial(_mpr_kernel, tn=tn),
        grid=((B // bb) * nt,),
        in_specs=[
            pl.BlockSpec((tm, K), lambda i: (i, 0)),
            pl.BlockSpec((K, D), lambda i: (0, 0), pipeline_mode=pl.Buffered(1)),
            xspec,
            pl.BlockSpec((bb, 1, D), lambda i: (i // nt, 0, 0)),
            pl.BlockSpec((1, D), lambda i: (0, 0)),
        ],
        out_specs=xspec,
        out_shape=jax.ShapeDtypeStruct((B, T, D), F32),
        scratch_shapes=[pltpu.VMEM((D // tn, tm, tn), F32)],
        compiler_params=_cparams(("parallel",)),
        name=name,
    )(a, w, x, gate, g.reshape(1, D))


def _rev_tri(n):
    j = lax.broadcasted_iota(jnp.int32, (n, n), 0)
    s = lax.broadcasted_iota(jnp.int32, (n, n), 1)
    return jnp.where(j >= s, 1.0, 0.0).astype(BF16)


def _sb_tile(q, kb, vb, tri, carry, mask):
    z2 = lax.dot_general(q, kb, (((1,), (1,)), ((), ())), preferred_element_type=F32)
    sp = _softplus2(z2)
    if mask is not None:
        sp = jnp.where(mask, sp, 0.0)
    hi = sp.astype(BF16)
    lo = (sp - hi.astype(F32)).astype(BF16)
    csum = _dot(hi, tri) + _dot(lo, tri)
    p = jnp.exp2(z2 - csum - carry)
    if mask is not None:
        p = jnp.where(mask, p, 0.0)
    return _dot(p.astype(BF16), vb), csum[:, 0:1]


def _attn_self_kernel(q_ref, zg_ref, k_ref, v_ref, o_ref, acc_ref, carry_ref,
                      z_ref, hi_ref, lo_ref, *, tq, tk, nh):
    i = pl.program_id(2)
    q_all = (q_ref[0] * QSCALE).astype(BF16)
    qs = [q_all[:, g * HEAD_DIM:(g + 1) * HEAD_DIM] for g in range(nh)]
    tri = _rev_tri(tk)
    nd = tq // tk
    n = i * nd

    def keys(ref, g, kt):
        ks = pl.multiple_of(kt * tk, tk)
        return ref[0, pl.ds(ks, tk), g * HEAD_DIM:(g + 1) * HEAD_DIM]

    def tile(g, kt, r0, mask):
        pv, tot = _sb_tile(qs[g][r0:], keys(k_ref, g, kt), keys(v_ref, g, kt), tri,
                           carry_ref[g, r0:, :], mask)
        acc_ref[g, r0:, :] += pv
        carry_ref[g, r0:, :] += tot

    def scores(g, kt, slot):
        z2 = lax.dot_general(qs[g], keys(k_ref, g, kt), (((1,), (1,)), ((), ())),
                             preferred_element_type=F32)
        sp = _softplus2(z2)
        hi = sp.astype(BF16)
        z_ref[g, slot] = z2
        hi_ref[g, slot] = hi
        lo_ref[g, slot] = (sp - hi.astype(F32)).astype(BF16)

    def weights(g, kt, slot):
        csum = _dot(hi_ref[g, slot], tri) + _dot(lo_ref[g, slot], tri)
        p = jnp.exp2(z_ref[g, slot] - csum - carry_ref[g])
        acc_ref[g] += _dot(p.astype(BF16), keys(v_ref, g, kt))
        carry_ref[g] += csum[:, 0:1]

    acc_ref[...] = jnp.zeros_like(acc_ref)
    carry_ref[...] = jnp.zeros_like(carry_ref)
    for g in range(nh):
        scores(g, jnp.maximum(n - 1, 0), 0)
    for c in reversed(range(nd)):
        rows = tq - c * tk
        row = lax.broadcasted_iota(jnp.int32, (rows, tk), 0)
        col = lax.broadcasted_iota(jnp.int32, (rows, tk), 1)
        for g in range(nh):
            tile(g, n + c, c * tk, col < row)

    def body(m, c):
        t = 2 * m
        for g in range(nh):
            scores(g, n - 2 - t, 1)
            weights(g, n - 1 - t, 0)
        for g in range(nh):
            scores(g, jnp.maximum(n - 3 - t, 0), 0)
            weights(g, n - 2 - t, 1)
        return c

    lax.fori_loop(0, n // 2, body, 0)
    acc = jnp.concatenate([acc_ref[g] for g in range(nh)], axis=1)
    o_ref[0] = (acc * _silu(zg_ref[0])).astype(o_ref.dtype)


def _attn_self_call(q, zg, k, v, *, tq, tk, nh):
    B, T, _ = q.shape
    assert (tq // tk) % 2 == 0 and tq % tk == 0 and T % tq == 0
    wide = nh * HEAD_DIM
    qspec = pl.BlockSpec((1, tq, wide), lambda b, h, i: (b, i, h))
    kspec = pl.BlockSpec((1, T, wide), lambda b, h, i: (b, 0, h))
    return pl.pallas_call(
        functools.partial(_attn_self_kernel, tq=tq, tk=tk, nh=nh),
        grid=(B, N_HEADS // nh, T // tq),
        in_specs=[qspec, qspec, kspec, kspec],
        out_specs=qspec,
        out_shape=jax.ShapeDtypeStruct((B, T, D_ATT), BF16),
        scratch_shapes=[pltpu.VMEM((nh, tq, HEAD_DIM), F32), pltpu.VMEM((nh, tq, 1), F32),
                        pltpu.VMEM((nh, 2, tq, tk), F32), pltpu.VMEM((nh, 2, tq, tk), BF16),
                        pltpu.VMEM((nh, 2, tq, tk), BF16)],
        compiler_params=_cparams(("parallel", "parallel", "arbitrary")),
        name="sb_attn_self",
    )(q, zg, k, v)


def _attn_past_kernel(q_ref, zg_ref, kn_ref, vn_ref, kc_ref, vc_ref, o_ref, acc_ref, carry_ref, *, T, tp, blk):
    j = pl.program_id(2)
    nh = SUBLANES
    nc = tp // blk
    q_all = (q_ref[0] * QSCALE).astype(BF16)

    @pl.when(j == 0)
    def _():
        lane = lax.broadcasted_iota(jnp.int32, q_all.shape, 1)
        q_bd = jnp.concatenate(
            [jnp.where((lane >= h * HEAD_DIM) & (lane < (h + 1) * HEAD_DIM), q_all, jnp.zeros_like(q_all))
             for h in range(nh)], axis=0)
        t_idx = jnp.concatenate([lax.broadcasted_iota(jnp.int32, (T, T), 0)] * nh, axis=0)
        s_idx = lax.broadcasted_iota(jnp.int32, (nh * T, T), 1)
        pv, tot = _sb_tile(q_bd, kn_ref[0].astype(BF16), vn_ref[0].astype(BF16), _rev_tri(T),
                           jnp.zeros((nh * T, 1), F32), s_idx < t_idx)
        for h in range(nh):
            acc_ref[h] = pv[h * T:(h + 1) * T, h * HEAD_DIM:(h + 1) * HEAD_DIM]
            carry_ref[h] = tot[h * T:(h + 1) * T]

    kc2 = kc_ref.reshape(tp * nh, HEAD_DIM)
    vc2 = vc_ref.reshape(tp * nh, HEAD_DIM)
    zs, lks = [], []
    for h in range(nh):
        kh = kc2[pl.ds(h, tp, stride=nh), :].astype(BF16)
        z = lax.dot_general(q_all[:, h * HEAD_DIM:(h + 1) * HEAD_DIM], kh, (((1,), (1,)), ((), ())),
                            preferred_element_type=F32)
        zs.append(z)
        lks.append(_softplus2(z))
    stack = jnp.concatenate([lk[:, c * blk:(c + 1) * blk] for lk in lks for c in range(nc)], axis=0)
    hi = stack.astype(BF16)
    lo = (stack - hi.astype(F32)).astype(BF16)
    tri = _rev_tri(blk)
    cs = _dot(hi, tri) + _dot(lo, tri)
    for h in range(nh):
        carry = carry_ref[h]
        ps = [None] * nc
        for c in reversed(range(nc)):
            r0 = (h * nc + c) * T
            csum = cs[r0:r0 + T]
            ps[c] = jnp.exp2(zs[h][:, c * blk:(c + 1) * blk] - csum - carry)
            carry = carry + csum[:, 0:1]
        p = jnp.concatenate(ps, axis=1).astype(BF16)
        vh = vc2[pl.ds(h, tp, stride=nh), :].astype(BF16)
        acc_ref[h] += _dot(p, vh)
        carry_ref[h] = carry

    @pl.when(j == pl.num_programs(2) - 1)
    def _():
        acc = jnp.concatenate([acc_ref[h] for h in range(nh)], axis=1)
        o_ref[0] = (acc * _silu(zg_ref[0])).astype(o_ref.dtype)


def _attn_past_call(q, zg, k_new, v_new, k_past, v_past, *, tp, blk):
    B, T, _ = q.shape
    P = k_past.shape[1]
    nh = SUBLANES
    ng = N_HEADS // nh
    nt = P // tp
    wide = nh * HEAD_DIM
    qspec = pl.BlockSpec((1, T, wide), lambda b, g, j: (b, 0, g))
    pspec = pl.BlockSpec((None, tp, None, nh, HEAD_DIM), lambda b, g, j: (b, nt - 1 - j, g, 0, 0))
    k5 = k_past.reshape(B, P, ng, nh, HEAD_DIM)
    v5 = v_past.reshape(B, P, ng, nh, HEAD_DIM)
    return pl.pallas_call(
        functools.partial(_attn_past_kernel, T=T, tp=tp, blk=blk),
        grid=(B, ng, nt),
        in_specs=[qspec, qspec, qspec, qspec, pspec, pspec],
        out_specs=qspec,
        out_shape=jax.ShapeDtypeStruct((B, T, D_ATT), BF16),
        scratch_shapes=[pltpu.VMEM((nh, T, HEAD_DIM), F32), pltpu.VMEM((nh, T, 1), F32)],
        compiler_params=_cparams(("parallel", "parallel", "arbitrary")),
        name="sb_attn_past",
    )(q, zg, k_new, v_new, k5, v5)


class _Plan(NamedTuple):
    in_rows: tuple
    out_rows: tuple
    lru_rows: tuple
    cols: int
    lru_blocks: int
    tq: int
    heads: int
    tp: int


def _plan(B, T):
    long_seq = T >= 4 * MXU_DIM
    return _Plan(
        in_rows=(1, 4 * MXU_DIM) if long_seq else (B, T),
        out_rows=(1, 2 * MXU_DIM) if long_seq else (B, T),
        lru_rows=(1, 2 * MXU_DIM) if long_seq else (B, T),
        cols=(2 if long_seq else 4) * MXU_DIM, lru_blocks=2, tq=2 * MXU_DIM, heads=4, tp=8 * MXU_DIM)


def _run_group(x, mods, h0, conv0, past_k, past_v, p):
    B, T, D = x.shape
    plan = _plan(B, T)
    in_bb, in_tt = plan.in_rows
    out_bb, out_tt = plan.out_rows
    lru_bb, lru_tt = plan.lru_rows
    hs, convs = [], []
    k_new = v_new = None
    for l in range(DEPTH):
        shift, scale, gate = mods[l]
        if l < N_A:
            xb, zg = _nmm_call(x, p["g_pre"][l], scale, shift, p["w_in_a"][l],
                               bb=in_bb, tt=in_tt, tn=plan.cols, name="in_proj_a")
            yz, ht, cn = _lru_call(xb.reshape(B, T, D_RNN), zg.reshape(B, T, D_RNN), conv0[l], h0[l],
                                   p["conv_w"][l], p["conv_b"][l], p["w_rgate"][l], p["b_rgate"][l],
                                   p["w_igate"][l], p["b_igate"][l], p["lru_lambda"][l],
                                   bb=lru_bb, tt=lru_tt, nblk=plan.lru_blocks)
            hs.append(ht.reshape(B, D_RNN))
            convs.append(cn)
            x = _mpr_call(yz.reshape(B * T, D_RNN), p["w_out_a"][l], x, gate, p["g_post"][l],
                          bb=out_bb, tt=out_tt, tn=plan.cols, name="out_proj_a")
        else:
            if l == N_A:
                zero = jnp.zeros_like(scale)
                k_new, v_new, k16, v16 = _nmm_call(x, p["g_kv"], zero, zero, p["w_kv"], bb=in_bb,
                                                   tt=in_tt, tn=plan.cols, name="kv_proj",
                                                   bf16_copies=True)
            j = l - N_A
            q, zg = _nmm_call(x, p["g_pre"][l], scale, shift, p["w_in_b"][j],
                              bb=in_bb, tt=in_tt, tn=plan.cols, name="in_proj_b")
            q3, zg3 = q.reshape(B, T, D_ATT), zg.reshape(B, T, D_ATT)
            k3, v3 = k16.reshape(B, T, D_ATT), v16.reshape(B, T, D_ATT)
            if past_k is None:
                oz = _attn_self_call(q3, zg3, k3, v3, tq=plan.tq, tk=MXU_DIM, nh=plan.heads)
            else:
                oz = _attn_past_call(q3, zg3, k3, v3, past_k, past_v, tp=plan.tp, blk=MXU_DIM)
            x = _mpr_call(oz.reshape(B * T, D_ATT), p["w_out_b"][j], x, gate, p["g_post"][l],
                          bb=out_bb, tt=out_tt, tn=plan.cols, name="out_proj_b")
    return (x, k_new.reshape(B, T, N_HEADS, HEAD_DIM), v_new.reshape(B, T, N_HEADS, HEAD_DIM),
            jnp.stack(hs), jnp.stack(convs))


def kernel(x_prompt, x_sample, c_prompt, c_sample, cache_k, cache_v, state_lru, state_conv, g_pre, g_post, w_ada, b_ada, w_in_a, conv_w, conv_b, w_rgate, b_rgate, w_igate, b_igate, lru_lambda, w_out_a, g_kv, w_kv, w_in_b, w_out_b):
    Bp, Bs = x_prompt.shape[0], x_sample.shape[0]
    p = dict(g_pre=g_pre, g_post=g_post, conv_w=conv_w, conv_b=conv_b, b_rgate=b_rgate, b_igate=b_igate,
             lru_lambda=lru_lambda, g_kv=g_kv,
             w_in_a=w_in_a.astype(BF16), w_rgate=w_rgate.astype(BF16), w_igate=w_igate.astype(BF16),
             w_out_a=w_out_a.astype(BF16), w_kv=w_kv.astype(BF16), w_in_b=w_in_b.astype(BF16),
             w_out_b=w_out_b.astype(BF16))

    rows = Bp + Bs
    rows_pad = -(-rows // SUBLANES) * SUBLANES
    c_all = jnp.concatenate([c_prompt, c_sample, jnp.zeros((rows_pad - rows, D_MODEL), F32)], axis=0)
    m = _ada_call(c_all, w_ada, b_ada)

    def mods_of(lo, hi):
        out = []
        for l in range(DEPTH):
            ml = m[l, lo:hi]
            out.append(tuple(ml[:, i * D_MODEL:(i + 1) * D_MODEL][:, None, :] for i in range(3)))
        return out

    h0_p = jnp.zeros((N_A, Bp, D_RNN), F32)
    conv0_p = jnp.zeros((N_A, Bp, CONV_W - 1, D_RNN), F32)
    y_p, k_p, v_p, lru_p, conv_p = _run_group(x_prompt, mods_of(0, Bp), h0_p, conv0_p, None, None, p)
    y_s, k_s, v_s, lru_s, conv_s = _run_group(x_sample, mods_of(Bp, rows), state_lru, state_conv,
                                              cache_k, cache_v, p)
    return (y_p, y_s, k_p, v_p, k_s, v_s, lru_p, lru_s, conv_p, conv_s)
```

```python
import functools
from typing import NamedTuple

import jax
import jax.numpy as jnp
from jax import lax
from jax.experimental import pallas as pl
from jax.experimental.pallas import tpu as pltpu

D_MODEL = 2048
DEPTH = 2
N_A = DEPTH // 2
N_B = DEPTH - N_A
D_RNN = 2 * D_MODEL
N_GATE_BLOCKS = 16
GATE_BLOCK = D_RNN // N_GATE_BLOCKS
CONV_W = 4
LRU_C = 8.0
N_HEADS = 16
HEAD_DIM = D_MODEL // N_HEADS
D_ATT = N_HEADS * HEAD_DIM
EPS = 1e-6
LOG2E = 1.4426950408889634
QSCALE = HEAD_DIM ** -0.5 * LOG2E

SUBLANES = 8
LANES = 128
MXU_DIM = 256
PROJ_COLS = 2 * MXU_DIM
VMEM_LIMIT = 56 * 1024 * 1024

F32 = jnp.float32
BF16 = jnp.bfloat16


def _cparams(sem):
    return pltpu.CompilerParams(dimension_semantics=sem, vmem_limit_bytes=VMEM_LIMIT)


def _dot(a, b):
    return jnp.dot(a, b, preferred_element_type=F32)


def _sigmoid(x):
    return 0.5 * jnp.tanh(0.5 * x) + 0.5


def _silu(x):
    h = 0.5 * x
    return h + h * jnp.tanh(h)


def _softplus(x):
    return jnp.maximum(x, 0.0) + jnp.log(1.0 + jnp.exp(-jnp.abs(x)))


def _softplus2(x2):
    return jnp.maximum(x2, 0.0) + jnp.log(1.0 + jnp.exp2(-jnp.abs(x2))) * LOG2E


def _ada_kernel(c_ref, w_ref, b_ref, o_ref):
    c = c_ref[...]
    s = _silu(c).astype(BF16)
    o_ref[...] = _dot(s, w_ref[...].astype(BF16)) + b_ref[...]


def _ada_call(c_all, w_ada, b_ada):
    rows = c_all.shape[0]
    tn = 3 * MXU_DIM
    nout = 3 * D_MODEL
    return pl.pallas_call(
        _ada_kernel,
        grid=(DEPTH, nout // tn),
        in_specs=[
            pl.BlockSpec((rows, D_MODEL), lambda l, j: (0, 0)),
            pl.BlockSpec((None, D_MODEL, tn), lambda l, j: (l, 0, j)),
            pl.BlockSpec((None, 1, tn), lambda l, j: (l, 0, j)),
        ],
        out_specs=pl.BlockSpec((None, rows, tn), lambda l, j: (l, 0, j)),
        out_shape=jax.ShapeDtypeStruct((DEPTH, rows, nout), F32),
        compiler_params=_cparams(("parallel", "parallel")),
        name="ada_mod",
    )(c_all, w_ada, b_ada.reshape(DEPTH, 1, nout))


def _nmm_kernel(x_ref, g_ref, sc_ref, sh_ref, wl_ref, wr_ref, *rest):
    *out_refs, hn_ref = rest

    @pl.when(pl.program_id(1) == 0)
    def _():
        x = x_ref[...]
        y = x * lax.rsqrt(jnp.mean(x * x, axis=-1, keepdims=True) + EPS)
        hn = y * (g_ref[...] * (1.0 + sc_ref[...])) + sh_ref[...]
        hn_ref[...] = hn.reshape(hn_ref.shape).astype(BF16)

    hn = hn_ref[...]
    left = _dot(hn, wl_ref[...])
    right = _dot(hn, wr_ref[...])
    for o_ref, val in zip(out_refs, (left, right, left, right)):
        o_ref[...] = val.astype(o_ref.dtype)


def _col_tiles(w):
    *lead, d, n = w.shape
    w = w.astype(BF16).reshape(*lead, d, n // PROJ_COLS, PROJ_COLS)
    return jnp.swapaxes(w, -3, -2)


def _nmm_call(x, g, scale, shift, w, *, bb, tt, name, bf16_copies=False):
    B, T, D = x.shape
    nj = w.shape[0] // 2
    tn = w.shape[2]
    nh = nj * tn
    nt = T // tt
    tm = bb * tt
    out = [jax.ShapeDtypeStruct((B * T, nh), F32)] * 2
    if bf16_copies:
        out = out + [jax.ShapeDtypeStruct((B * T, nh), BF16)] * 2
    return pl.pallas_call(
        _nmm_kernel,
        grid=((B // bb) * nt, nj),
        in_specs=[
            pl.BlockSpec((bb, tt, D), lambda i, j: (i // nt, i % nt, 0)),
            pl.BlockSpec((1, 1, D), lambda i, j: (0, 0, 0)),
            pl.BlockSpec((bb, 1, D), lambda i, j: (i // nt, 0, 0)),
            pl.BlockSpec((bb, 1, D), lambda i, j: (i // nt, 0, 0)),
            pl.BlockSpec((None, D, tn), lambda i, j: (j, 0, 0)),
            pl.BlockSpec((None, D, tn), lambda i, j: (j + nj, 0, 0)),
        ],
        out_specs=[pl.BlockSpec((tm, tn), lambda i, j: (i, j))] * len(out),
        out_shape=out,
        scratch_shapes=[pltpu.VMEM((tm, D), BF16)],
        compiler_params=_cparams(("parallel", "arbitrary")),
        name=name,
    )(x, g.reshape(1, 1, D), scale, shift, w, w)


def _lru_coeffs(x2, wr_ref, br_ref, wi_ref, bi_ref, lam_ref, nblk):
    x16 = x2.astype(BF16)
    rg, ig = [], []
    for k in range(nblk):
        xk = x16[:, k * GATE_BLOCK:(k + 1) * GATE_BLOCK]
        rg.append(_dot(xk, wr_ref[k]))
        ig.append(_dot(xk, wi_ref[k]))
    ig = _sigmoid(jnp.concatenate(ig, axis=1) + bi_ref[...])
    half = (-0.5 * LRU_C) * _softplus(-lam_ref[...])
    log_a = jnp.tanh(0.5 * (jnp.concatenate(rg, axis=1) + br_ref[...])) * half + half
    a = jnp.exp(log_a)
    w = -jnp.tanh(log_a) * (a * a + 1.0)
    u = jnp.exp2((0.5 * LOG2E) * jnp.log(w)) * (ig * x2)
    return a, u


def _lru_steps_kernel(xb_ref, zg_ref, conv0_ref, h0_ref, cw_ref, cbias_ref, wr_ref, br_ref, wi_ref,
                      bi_ref, lam_ref, yz_ref, ht_ref, cn_ref, xp_ref, hs_ref, car_ref, hin_ref, h_ref,
                      *, tt, nblk):
    pad = SUBLANES
    hist = CONV_W - 1
    width = nblk * GATE_BLOCK
    nslab = width // LANES
    M = tt // SUBLANES

    def lanes(s):
        return slice(s * LANES, (s + 1) * LANES)

    @pl.when(pl.program_id(2) == 0)
    def _():
        for s in range(nslab):
            xp_ref[s, pad - hist:pad, :] = conv0_ref[0, :, lanes(s)]
        h_ref[...] = h0_ref[...]

    for s in range(nslab):
        xp_ref[s, pad:, :] = xb_ref[0, :, lanes(s)]
    step = {m: [xp_ref[s, pl.ds(pad + m, M, stride=SUBLANES), :] for s in range(nslab)]
            for m in range(-hist, SUBLANES)}
    conv = []
    for k in range(SUBLANES):
        cols = [cbias_ref[:, lanes(s)]
                + sum(step[k - hist + i][s] * cw_ref[i:i + 1, lanes(s)] for i in range(CONV_W))
                for s in range(nslab)]
        conv.append(jnp.concatenate(cols, axis=-1))
    cn_ref[0] = jnp.concatenate([xp_ref[s, pad + tt - hist:pad + tt, :] for s in range(nslab)], axis=-1)
    for s in range(nslab):
        xp_ref[s, 0:pad, :] = xp_ref[s, tt:tt + pad, :]

    a, u = _lru_coeffs(jnp.concatenate(conv, axis=0), wr_ref, br_ref, wi_ref, bi_ref, lam_ref, nblk)

    a_run, u_run = a[0:M], u[0:M]
    a_steps, u_steps = [a_run], [u_run]
    for k in range(1, SUBLANES):
        a_k = a[k * M:(k + 1) * M]
        u_run = a_k * u_run + u[k * M:(k + 1) * M]
        a_run = a_k * a_run
        a_steps.append(a_run)
        u_steps.append(u_run)

    car_ref[0] = a_run
    car_ref[1] = u_run
    hin_ref[0:1, :] = h_ref[0]

    def body(g, h):
        h = car_ref[0, pl.ds(g, 1), :] * h + car_ref[1, pl.ds(g, 1), :]
        hin_ref[pl.ds(g + 1, 1), :] = h
        return h

    h_last = lax.fori_loop(0, M, body, h_ref[0])
    h_ref[0] = h_last
    ht_ref[0] = h_last
    h_in = hin_ref[0:M, :]
    for k in range(SUBLANES):
        h_k = a_steps[k] * h_in + u_steps[k]
        for s in range(nslab):
            hs_ref[s, pl.ds(k, M, stride=SUBLANES), :] = h_k[:, lanes(s)]
    y = jnp.concatenate([hs_ref[s] for s in range(nslab)], axis=-1)
    yz_ref[0] = (y * _silu(zg_ref[0])).astype(yz_ref.dtype)


def _lru_kernel(xb_ref, zg_ref, conv0_ref, h0_ref, cw_ref, cbias_ref, wr_ref, br_ref, wi_ref, bi_ref,
                lam_ref, yz_ref, ht_ref, cn_ref, xp_ref, a_ref, u_ref, h_ref, *, bb, tt, nblk):
    pad = SUBLANES
    hist = CONV_W - 1

    @pl.when(pl.program_id(2) == 0)
    def _():
        xp_ref[:, pad - hist:pad, :] = conv0_ref[...]
        h_ref[...] = h0_ref[...]

    xp_ref[:, pad:, :] = xb_ref[...]
    xc = cbias_ref[...] + sum(
        xp_ref[:, pad - hist + i:pad - hist + i + tt, :] * cw_ref[i:i + 1, :] for i in range(CONV_W))
    cn_ref[...] = xp_ref[:, pad + tt - hist:pad + tt, :]
    xp_ref[:, 0:pad, :] = xp_ref[:, tt:tt + pad, :]

    width = nblk * GATE_BLOCK
    a, u = _lru_coeffs(xc.reshape(bb * tt, width), wr_ref, br_ref, wi_ref, bi_ref, lam_ref, nblk)

    nslab = width // LANES
    for s in range(nslab):
        a_ref[s] = a[:, s * LANES:(s + 1) * LANES]
        u_ref[s] = u[:, s * LANES:(s + 1) * LANES]

    ngroups = bb * tt // SUBLANES

    def step_rows(k):
        return pl.ds(k, ngroups, stride=SUBLANES)

    a_run = a_ref[:, step_rows(0), :]
    u_run = u_ref[:, step_rows(0), :]
    for k in range(1, SUBLANES):
        a_k = a_ref[:, step_rows(k), :]
        u_run = a_k * u_run + u_ref[:, step_rows(k), :]
        a_run = a_k * a_run
        a_ref[:, step_rows(k), :] = a_run
        u_ref[:, step_rows(k), :] = u_run

    h_in = jnp.stack([h_ref[:, :, s * LANES:(s + 1) * LANES] for s in range(nslab)], axis=0)
    if bb == 1:
        def body(gidx, h):
            s = pl.multiple_of(gidx * SUBLANES, SUBLANES)
            hg = a_ref[:, pl.ds(s, SUBLANES), :] * h + u_ref[:, pl.ds(s, SUBLANES), :]
            u_ref[:, pl.ds(s, SUBLANES), :] = hg
            return hg[:, SUBLANES - 1:SUBLANES, :]

        h_out = lax.fori_loop(0, tt // SUBLANES, body, h_in.reshape(nslab, 1, LANES))
        h_out = h_out.reshape(nslab, 1, 1, LANES)
        hs = u_ref[...]
    else:
        a4 = a_ref[...].reshape(nslab, bb, tt, LANES)
        u4 = u_ref[...].reshape(nslab, bb, tt, LANES)
        h_out, parts = h_in, []
        for gidx in range(tt // SUBLANES):
            sl = slice(gidx * SUBLANES, (gidx + 1) * SUBLANES)
            hg = a4[:, :, sl] * h_out + u4[:, :, sl]
            parts.append(hg)
            h_out = hg[:, :, SUBLANES - 1:SUBLANES]
        hs = jnp.concatenate(parts, axis=2).reshape(nslab, bb * tt, LANES)

    h_last = jnp.concatenate([h_out[s] for s in range(nslab)], axis=-1)
    h_ref[...] = h_last
    ht_ref[...] = h_last
    y = jnp.concatenate([hs[s] for s in range(nslab)], axis=-1).reshape(bb, tt, width)
    yz_ref[...] = (y * _silu(zg_ref[...])).astype(yz_ref.dtype)


def _lru_call(xb, zg, conv0, h0, conv_w, conv_b, w_r, b_r, w_i, b_i, lam, *, bb, tt, nblk):
    B, T, Dr = xb.shape
    C = nblk * GATE_BLOCK
    hist = CONV_W - 1
    big = pl.BlockSpec((bb, tt, C), lambda b, c, t: (b, t, c))
    vec = pl.BlockSpec((1, C), lambda b, c, t: (0, c))
    gate_w = pl.BlockSpec((nblk, GATE_BLOCK, GATE_BLOCK), lambda b, c, t: (c, 0, 0))
    state3 = pl.BlockSpec((bb, hist, C), lambda b, c, t: (b, 0, c))
    state1 = pl.BlockSpec((bb, 1, C), lambda b, c, t: (b, 0, c))
    nslab = C // LANES
    if bb == 1:
        body = functools.partial(_lru_steps_kernel, tt=tt, nblk=nblk)
        scratch = [pltpu.VMEM((nslab, tt + SUBLANES, LANES), F32),
                   pltpu.VMEM((nslab, tt, LANES), F32),
                   pltpu.VMEM((2, tt // SUBLANES, C), F32),
                   pltpu.VMEM((tt // SUBLANES + SUBLANES, C), F32),
                   pltpu.VMEM((1, 1, C), F32)]
    else:
        body = functools.partial(_lru_kernel, bb=bb, tt=tt, nblk=nblk)
        scratch = [pltpu.VMEM((bb, tt + SUBLANES, C), F32),
                   pltpu.VMEM((nslab, bb * tt, LANES), F32),
                   pltpu.VMEM((nslab, bb * tt, LANES), F32),
                   pltpu.VMEM((bb, 1, C), F32)]
    return pl.pallas_call(
        body,
        grid=(B // bb, Dr // C, T // tt),
        in_specs=[big, big, state3, state1,
                  pl.BlockSpec((CONV_W, C), lambda b, c, t: (0, c)), vec,
                  gate_w, vec, gate_w, vec, vec],
        out_specs=[big, state1, state3],
        out_shape=[jax.ShapeDtypeStruct((B, T, Dr), BF16),
                   jax.ShapeDtypeStruct((B, 1, Dr), F32),
                   jax.ShapeDtypeStruct((B, hist, Dr), F32)],
        scratch_shapes=scratch,
        compiler_params=_cparams(("parallel", "parallel", "arbitrary")),
        name="rglru",
    )(xb, zg, conv0, h0.reshape(B, 1, Dr), conv_w, conv_b.reshape(1, Dr), w_r, b_r.reshape(1, Dr),
      w_i, b_i.reshape(1, Dr), lam.reshape(1, Dr))


def _mpr_kernel(a_ref, w_ref, x_ref, gate_ref, g_ref, o_ref, out_ref, *, tn):
    a = a_ref[...]
    for c in range(out_ref.shape[0]):
        out_ref[c] = _dot(a, w_ref[:, c * tn:(c + 1) * tn])
    out = jnp.concatenate([out_ref[c] for c in range(out_ref.shape[0])], axis=1)
    n = out * lax.rsqrt(jnp.mean(out * out, axis=-1, keepdims=True) + EPS) * g_ref[...]
    o_ref[...] = x_ref[...] + gate_ref[...] * n.reshape(o_ref.shape)


def _mpr_call(a, w, x, gate, g, *, bb, tt, tn, name):
    B, T, D = x.shape
    K = a.shape[1]
    nt = T // tt
    tm = bb * tt
    xspec = pl.BlockSpec((bb, tt, D), lambda i: (i // nt, i % nt, 0))
    return pl.pallas_call(
        functools.partial(_mpr_kernel, tn=tn),
        grid=((B // bb) * nt,),
        in_specs=[
            pl.BlockSpec((tm, K), lambda i: (i, 0)),
            pl.BlockSpec((K, D), lambda i: (0, 0), pipeline_mode=pl.Buffered(1)),
            xspec,
            pl.BlockSpec((bb, 1, D), lambda i: (i // nt, 0, 0)),
            pl.BlockSpec((1, D), lambda i: (0, 0)),
        ],
        out_specs=xspec,
        out_shape=jax.ShapeDtypeStruct((B, T, D), F32),
        scratch_shapes=[pltpu.VMEM((D // tn, tm, tn), F32)],
        compiler_params=_cparams(("parallel",)),
        name=name,
    )(a, w, x, gate, g.reshape(1, D))


def _rev_tri(n):
    j = lax.broadcasted_iota(jnp.int32, (n, n), 0)
    s = lax.broadcasted_iota(jnp.int32, (n, n), 1)
    return jnp.where(j >= s, 1.0, 0.0).astype(BF16)


def _sb_tile(q, kb, vb, tri, carry, mask):
    z2 = lax.dot_general(q, kb, (((1,), (1,)), ((), ())), preferred_element_type=F32)
    sp = _softplus2(z2)
    if mask is not None:
        sp = jnp.where(mask, sp, 0.0)
    hi = sp.astype(BF16)
    lo = (sp - hi.astype(F32)).astype(BF16)
    csum = _dot(hi, tri) + _dot(lo, tri)
    p = jnp.exp2(z2 - csum - carry)
    if mask is not None:
        p = jnp.where(mask, p, 0.0)
    return _dot(p.astype(BF16), vb), csum[:, 0:1]


def _attn_self_kernel(q_ref, zg_ref, k_ref, v_ref, o_ref, acc_ref, carry_ref,
                      z_ref, hi_ref, lo_ref, *, tq, tk, nh):
    i = pl.program_id(2)
    q_all = (q_ref[0] * QSCALE).astype(BF16)
    qs = [q_all[:, g * HEAD_DIM:(g + 1) * HEAD_DIM] for g in range(nh)]
    tri = _rev_tri(tk)
    nd = tq // tk
    n = i * nd

    def keys(ref, g, kt):
        ks = pl.multiple_of(kt * tk, tk)
        return ref[0, pl.ds(ks, tk), g * HEAD_DIM:(g + 1) * HEAD_DIM]

    def tile(g, kt, r0, mask):
        pv, tot = _sb_tile(qs[g][r0:], keys(k_ref, g, kt), keys(v_ref, g, kt), tri,
                           carry_ref[g, r0:, :], mask)
        acc_ref[g, r0:, :] += pv
        carry_ref[g, r0:, :] += tot

    def scores(g, kt, slot):
        z2 = lax.dot_general(qs[g], keys(k_ref, g, kt), (((1,), (1,)), ((), ())),
                             preferred_element_type=F32)
        sp = _softplus2(z2)
        hi = sp.astype(BF16)
        z_ref[g, slot] = z2
        hi_ref[g, slot] = hi
        lo_ref[g, slot] = (sp - hi.astype(F32)).astype(BF16)

    def weights(g, kt, slot):
        csum = _dot(hi_ref[g, slot], tri) + _dot(lo_ref[g, slot], tri)
        p = jnp.exp2(z_ref[g, slot] - csum - carry_ref[g])
        acc_ref[g] += _dot(p.astype(BF16), keys(v_ref, g, kt))
        carry_ref[g] += csum[:, 0:1]

    acc_ref[...] = jnp.zeros_like(acc_ref)
    carry_ref[...] = jnp.zeros_like(carry_ref)
    for g in range(nh):
        scores(g, jnp.maximum(n - 1, 0), 0)
    for c in reversed(range(nd)):
        rows = tq - c * tk
        row = lax.broadcasted_iota(jnp.int32, (rows, tk), 0)
        col = lax.broadcasted_iota(jnp.int32, (rows, tk), 1)
        for g in range(nh):
            tile(g, n + c, c * tk, col < row)

    def body(m, c):
        t = 2 * m
        for g in range(nh):
            scores(g, n - 2 - t, 1)
            weights(g, n - 1 - t, 0)
        for g in range(nh):
            scores(g, jnp.maximum(n - 3 - t, 0), 0)
            weights(g, n - 2 - t, 1)
        return c

    lax.fori_loop(0, n // 2, body, 0)
    acc = jnp.concatenate([acc_ref[g] for g in range(nh)], axis=1)
    o_ref[0] = (acc * _silu(zg_ref[0])).astype(o_ref.dtype)


def _attn_self_call(q, zg, k, v, *, tq, tk, nh):
    B, T, _ = q.shape
    assert (tq // tk) % 2 == 0 and tq % tk == 0 and T % tq == 0
    wide = nh * HEAD_DIM
    qspec = pl.BlockSpec((1, tq, wide), lambda b, h, i: (b, i, h))
    kspec = pl.BlockSpec((1, T, wide), lambda b, h, i: (b, 0, h))
    return pl.pallas_call(
        functools.partial(_attn_self_kernel, tq=tq, tk=tk, nh=nh),
        grid=(B, N_HEADS // nh, T // tq),
        in_specs=[qspec, qspec, kspec, kspec],
        out_specs=qspec,
        out_shape=jax.ShapeDtypeStruct((B, T, D_ATT), BF16),
        scratch_shapes=[pltpu.VMEM((nh, tq, HEAD_DIM), F32), pltpu.VMEM((nh, tq, 1), F32),
                        pltpu.VMEM((nh, 2, tq, tk), F32), pltpu.VMEM((nh, 2, tq, tk), BF16),
                        pltpu.VMEM((nh, 2, tq, tk), BF16)],
        compiler_params=_cparams(("parallel", "parallel", "arbitrary")),
        name="sb_attn_self",
    )(q, zg, k, v)


def _attn_past_kernel(q_ref, zg_ref, kn_ref, vn_ref, kc_ref, vc_ref, o_ref, acc_ref, carry_ref, *, T, tp, blk):
    j = pl.program_id(2)
    nh = SUBLANES
    nc = tp // blk
    q_all = (q_ref[0] * QSCALE).astype(BF16)

    @pl.when(j == 0)
    def _():
        lane = lax.broadcasted_iota(jnp.int32, q_all.shape, 1)
        q_bd = jnp.concatenate(
            [jnp.where((lane >= h * HEAD_DIM) & (lane < (h + 1) * HEAD_DIM), q_all, jnp.zeros_like(q_all))
             for h in range(nh)], axis=0)
        t_idx = jnp.concatenate([lax.broadcasted_iota(jnp.int32, (T, T), 0)] * nh, axis=0)
        s_idx = lax.broadcasted_iota(jnp.int32, (nh * T, T), 1)
        pv, tot = _sb_tile(q_bd, kn_ref[0].astype(BF16), vn_ref[0].astype(BF16), _rev_tri(T),
                           jnp.zeros((nh * T, 1), F32), s_idx < t_idx)
        for h in range(nh):
            acc_ref[h] = pv[h * T:(h + 1) * T, h * HEAD_DIM:(h + 1) * HEAD_DIM]
            carry_ref[h] = tot[h * T:(h + 1) * T]

    kc2 = kc_ref.reshape(tp * nh, HEAD_DIM)
    vc2 = vc_ref.reshape(tp * nh, HEAD_DIM)
    zs, lks = [], []
    for h in range(nh):
        kh = kc2[pl.ds(h, tp, stride=nh), :].astype(BF16)
        z = lax.dot_general(q_all[:, h * HEAD_DIM:(h + 1) * HEAD_DIM], kh, (((1,), (1,)), ((), ())),
                            preferred_element_type=F32)
        zs.append(z)
        lks.append(_softplus2(z))
    stack = jnp.concatenate([lk[:, c * blk:(c + 1) * blk] for lk in lks for c in range(nc)], axis=0)
    hi = stack.astype(BF16)
    lo = (stack - hi.astype(F32)).astype(BF16)
    tri = _rev_tri(blk)
    cs = _dot(hi, tri) + _dot(lo, tri)
    for h in range(nh):
        carry = carry_ref[h]
        ps = [None] * nc
        for c in reversed(range(nc)):
            r0 = (h * nc + c) * T
            csum = cs[r0:r0 + T]
            ps[c] = jnp.exp2(zs[h][:, c * blk:(c + 1) * blk] - csum - carry)
            carry = carry + csum[:, 0:1]
        p = jnp.concatenate(ps, axis=1).astype(BF16)
        vh = vc2[pl.ds(h, tp, stride=nh), :].astype(BF16)
        acc_ref[h] += _dot(p, vh)
        carry_ref[h] = carry

    @pl.when(j == pl.num_programs(2) - 1)
    def _():
        acc = jnp.concatenate([acc_ref[h] for h in range(nh)], axis=1)
        o_ref[0] = (acc * _silu(zg_ref[0])).astype(o_ref.dtype)


def _attn_past_call(q, zg, k_new, v_new, k_past, v_past, *, tp, blk):
    B, T, _ = q.shape
    P = k_past.shape[1]
    nh = SUBLANES
    ng = N_HEADS // nh
    nt = P // tp
    wide = nh * HEAD_DIM
    qspec = pl.BlockSpec((1, T, wide), lambda b, g, j: (b, 0, g))
    pspec = pl.BlockSpec((None, tp, None, nh, HEAD_DIM), lambda b, g, j: (b, nt - 1 - j, g, 0, 0))
    k5 = k_past.reshape(B, P, ng, nh, HEAD_DIM)
    v5 = v_past.reshape(B, P, ng, nh, HEAD_DIM)
    return pl.pallas_call(
        functools.partial(_attn_past_kernel, T=T, tp=tp, blk=blk),
        grid=(B, ng, nt),
        in_specs=[qspec, qspec, qspec, qspec, pspec, pspec],
        out_specs=qspec,
        out_shape=jax.ShapeDtypeStruct((B, T, D_ATT), BF16),
        scratch_shapes=[pltpu.VMEM((nh, T, HEAD_DIM), F32), pltpu.VMEM((nh, T, 1), F32)],
        compiler_params=_cparams(("parallel", "parallel", "arbitrary")),
        name="sb_attn_past",
    )(q, zg, k_new, v_new, k5, v5)


class _Plan(NamedTuple):
    in_rows: tuple
    out_rows: tuple
    lru_rows: tuple
    lru_blocks: int
    tq: int
    heads: int
    tp: int


def _plan(B, T):
    long_seq = T >= 4 * MXU_DIM
    return _Plan(
        in_rows=(1, 4 * MXU_DIM) if long_seq else (B, T),
        out_rows=(1, 2 * MXU_DIM) if long_seq else (B, T),
        lru_rows=(1, 2 * MXU_DIM) if long_seq else (B, T),
        lru_blocks=2, tq=2 * MXU_DIM, heads=4, tp=8 * MXU_DIM)


def _run_group(x, mods, h0, conv0, past_k, past_v, p):
    B, T, D = x.shape
    plan = _plan(B, T)
    in_bb, in_tt = plan.in_rows
    out_bb, out_tt = plan.out_rows
    lru_bb, lru_tt = plan.lru_rows
    hs, convs = [], []
    k_new = v_new = None
    for l in range(DEPTH):
        shift, scale, gate = mods[l]
        if l < N_A:
            xb, zg = _nmm_call(x, p["g_pre"][l], scale, shift, p["w_in_a"][l],
                               bb=in_bb, tt=in_tt, name="in_proj_a")
            yz, ht, cn = _lru_call(xb.reshape(B, T, D_RNN), zg.reshape(B, T, D_RNN), conv0[l], h0[l],
                                   p["conv_w"][l], p["conv_b"][l], p["w_rgate"][l], p["b_rgate"][l],
                                   p["w_igate"][l], p["b_igate"][l], p["lru_lambda"][l],
                                   bb=lru_bb, tt=lru_tt, nblk=plan.lru_blocks)
            hs.append(ht.reshape(B, D_RNN))
            convs.append(cn)
            x = _mpr_call(yz.reshape(B * T, D_RNN), p["w_out_a"][l], x, gate, p["g_post"][l],
                          bb=out_bb, tt=out_tt, tn=PROJ_COLS, name="out_proj_a")
        else:
            if l == N_A:
                zero = jnp.zeros_like(scale)
                k_new, v_new, k16, v16 = _nmm_call(x, p["g_kv"], zero, zero, p["w_kv"], bb=in_bb,
                                                   tt=in_tt, name="kv_proj",
                                                   bf16_copies=True)
            j = l - N_A
            q, zg = _nmm_call(x, p["g_pre"][l], scale, shift, p["w_in_b"][j],
                              bb=in_bb, tt=in_tt, name="in_proj_b")
            q3, zg3 = q.reshape(B, T, D_ATT), zg.reshape(B, T, D_ATT)
            k3, v3 = k16.reshape(B, T, D_ATT), v16.reshape(B, T, D_ATT)
            if past_k is None:
                oz = _attn_self_call(q3, zg3, k3, v3, tq=plan.tq, tk=MXU_DIM, nh=plan.heads)
            else:
                oz = _attn_past_call(q3, zg3, k3, v3, past_k, past_v, tp=plan.tp, blk=MXU_DIM)
            x = _mpr_call(oz.reshape(B * T, D_ATT), p["w_out_b"][j], x, gate, p["g_post"][l],
                          bb=out_bb, tt=out_tt, tn=PROJ_COLS, name="out_proj_b")
    return (x, k_new.reshape(B, T, N_HEADS, HEAD_DIM), v_new.reshape(B, T, N_HEADS, HEAD_DIM),
            jnp.stack(hs), jnp.stack(convs))


def kernel(x_prompt, x_sample, c_prompt, c_sample, cache_k, cache_v, state_lru, state_conv, g_pre, g_post, w_ada, b_ada, w_in_a, conv_w, conv_b, w_rgate, b_rgate, w_igate, b_igate, lru_lambda, w_out_a, g_kv, w_kv, w_in_b, w_out_b):
    Bp, Bs = x_prompt.shape[0], x_sample.shape[0]
    p = dict(g_pre=g_pre, g_post=g_post, conv_w=conv_w, conv_b=conv_b, b_rgate=b_rgate, b_igate=b_igate,
             lru_lambda=lru_lambda, g_kv=g_kv,
             w_in_a=_col_tiles(w_in_a), w_rgate=w_rgate.astype(BF16), w_igate=w_igate.astype(BF16),
             w_out_a=w_out_a.astype(BF16), w_kv=_col_tiles(w_kv), w_in_b=_col_tiles(w_in_b),
             w_out_b=w_out_b.astype(BF16))

    rows = Bp + Bs
    rows_pad = -(-rows // SUBLANES) * SUBLANES
    c_all = jnp.concatenate([c_prompt, c_sample, jnp.zeros((rows_pad - rows, D_MODEL), F32)], axis=0)
    m = _ada_call(c_all, w_ada, b_ada)

    def mods_of(lo, hi):
        out = []
        for l in range(DEPTH):
            ml = m[l, lo:hi]
            out.append(tuple(ml[:, i * D_MODEL:(i + 1) * D_MODEL][:, None, :] for i in range(3)))
        return out

    h0_p = jnp.zeros((N_A, Bp, D_RNN), F32)
    conv0_p = jnp.zeros((N_A, Bp, CONV_W - 1, D_RNN), F32)
    y_p, k_p, v_p, lru_p, conv_p = _run_group(x_prompt, mods_of(0, Bp), h0_p, conv0_p, None, None, p)
    y_s, k_s, v_s, lru_s, conv_s = _run_group(x_sample, mods_of(Bp, rows), state_lru, state_conv,
                                              cache_k, cache_v, p)
    return (y_p, y_s, k_p, v_p, k_s, v_s, lru_p, lru_s, conv_p, conv_s)
```

```python
import functools
from typing import NamedTuple

import jax
import jax.numpy as jnp
from jax import lax
from jax.experimental import pallas as pl
from jax.experimental.pallas import tpu as pltpu

D_MODEL = 2048
DEPTH = 2
N_A = DEPTH // 2
N_B = DEPTH - N_A
D_RNN = 2 * D_MODEL
N_GATE_BLOCKS = 16
GATE_BLOCK = D_RNN // N_GATE_BLOCKS
CONV_W = 4
LRU_C = 8.0
N_HEADS = 16
HEAD_DIM = D_MODEL // N_HEADS
D_ATT = N_HEADS * HEAD_DIM
EPS = 1e-6
LOG2E = 1.4426950408889634
QSCALE = HEAD_DIM ** -0.5 * LOG2E

SUBLANES = 8
LANES = 128
MXU_DIM = 256
VMEM_LIMIT = 56 * 1024 * 1024

F32 = jnp.float32
BF16 = jnp.bfloat16


def _cparams(sem):
    return pltpu.CompilerParams(dimension_semantics=sem, vmem_limit_bytes=VMEM_LIMIT)


def _dot(a, b):
    return jnp.dot(a, b, preferred_element_type=F32)


def _sigmoid(x):
    return 0.5 * jnp.tanh(0.5 * x) + 0.5


def _silu(x):
    h = 0.5 * x
    return h + h * jnp.tanh(h)


def _softplus(x):
    return jnp.maximum(x, 0.0) + jnp.log(1.0 + jnp.exp(-jnp.abs(x)))


def _softplus2(x2):
    return jnp.maximum(x2, 0.0) + jnp.log(1.0 + jnp.exp2(-jnp.abs(x2))) * LOG2E


def _ada_kernel(c_ref, w_ref, b_ref, o_ref):
    c = c_ref[...]
    s = _silu(c).astype(BF16)
    o_ref[...] = _dot(s, w_ref[...].astype(BF16)) + b_ref[...]


def _ada_call(c_all, w_ada, b_ada):
    rows = c_all.shape[0]
    tn = 3 * MXU_DIM
    nout = 3 * D_MODEL
    return pl.pallas_call(
        _ada_kernel,
        grid=(DEPTH, nout // tn),
        in_specs=[
            pl.BlockSpec((rows, D_MODEL), lambda l, j: (0, 0)),
            pl.BlockSpec((None, D_MODEL, tn), lambda l, j: (l, 0, j)),
            pl.BlockSpec((None, 1, tn), lambda l, j: (l, 0, j)),
        ],
        out_specs=pl.BlockSpec((None, rows, tn), lambda l, j: (l, 0, j)),
        out_shape=jax.ShapeDtypeStruct((DEPTH, rows, nout), F32),
        compiler_params=_cparams(("parallel", "parallel")),
        name="ada_mod",
    )(c_all, w_ada, b_ada.reshape(DEPTH, 1, nout))


def _nmm_kernel(x_ref, g_ref, sc_ref, sh_ref, wl_ref, wr_ref, *rest):
    *out_refs, hn_ref = rest

    @pl.when(pl.program_id(1) == 0)
    def _():
        x = x_ref[...]
        y = x * lax.rsqrt(jnp.mean(x * x, axis=-1, keepdims=True) + EPS)
        hn = y * (g_ref[...] * (1.0 + sc_ref[...])) + sh_ref[...]
        hn_ref[...] = hn.reshape(hn_ref.shape).astype(BF16)

    hn = hn_ref[...]
    left = _dot(hn, wl_ref[...])
    right = _dot(hn, wr_ref[...])
    for o_ref, val in zip(out_refs, (left, right, left, right)):
        o_ref[...] = val.astype(o_ref.dtype)


def _nmm_call(x, g, scale, shift, w, *, bb, tt, tn, name, bf16_copies=False):
    B, T, D = x.shape
    nh = w.shape[1] // 2
    nt = T // tt
    nj = nh // tn
    tm = bb * tt
    out = [jax.ShapeDtypeStruct((B * T, nh), F32)] * 2
    if bf16_copies:
        out = out + [jax.ShapeDtypeStruct((B * T, nh), BF16)] * 2
    return pl.pallas_call(
        _nmm_kernel,
        grid=((B // bb) * nt, nj),
        in_specs=[
            pl.BlockSpec((bb, tt, D), lambda i, j: (i // nt, i % nt, 0)),
            pl.BlockSpec((1, 1, D), lambda i, j: (0, 0, 0)),
            pl.BlockSpec((bb, 1, D), lambda i, j: (i // nt, 0, 0)),
            pl.BlockSpec((bb, 1, D), lambda i, j: (i // nt, 0, 0)),
            pl.BlockSpec((D, tn), lambda i, j: (0, j)),
            pl.BlockSpec((D, tn), lambda i, j: (0, j + nj)),
        ],
        out_specs=[pl.BlockSpec((tm, tn), lambda i, j: (i, j))] * len(out),
        out_shape=out,
        scratch_shapes=[pltpu.VMEM((tm, D), BF16)],
        compiler_params=_cparams(("parallel", "arbitrary")),
        name=name,
    )(x, g.reshape(1, 1, D), scale, shift, w, w)


def _lru_coeffs(x2, wr_ref, br_ref, wi_ref, bi_ref, lam_ref, nblk):
    x16 = x2.astype(BF16)
    rg, ig = [], []
    for k in range(nblk):
        xk = x16[:, k * GATE_BLOCK:(k + 1) * GATE_BLOCK]
        rg.append(_dot(xk, wr_ref[k]))
        ig.append(_dot(xk, wi_ref[k]))
    ig = _sigmoid(jnp.concatenate(ig, axis=1) + bi_ref[...])
    half = (-0.5 * LRU_C) * _softplus(-lam_ref[...])
    log_a = jnp.tanh(0.5 * (jnp.concatenate(rg, axis=1) + br_ref[...])) * half + half
    a = jnp.exp(log_a)
    w = -jnp.tanh(log_a) * (a * a + 1.0)
    u = jnp.exp2((0.5 * LOG2E) * jnp.log(w)) * (ig * x2)
    return a, u


def _lru_steps_kernel(xb_ref, zg_ref, conv0_ref, h0_ref, cw_ref, cbias_ref, wr_ref, br_ref, wi_ref,
                      bi_ref, lam_ref, yz_ref, ht_ref, cn_ref, xp_ref, hs_ref, car_ref, hin_ref, h_ref,
                      *, tt, nblk):
    pad = SUBLANES
    hist = CONV_W - 1
    width = nblk * GATE_BLOCK
    nslab = width // LANES
    M = tt // SUBLANES

    def lanes(s):
        return slice(s * LANES, (s + 1) * LANES)

    @pl.when(pl.program_id(2) == 0)
    def _():
        for s in range(nslab):
            xp_ref[s, pad - hist:pad, :] = conv0_ref[0, :, lanes(s)]
        h_ref[...] = h0_ref[...]

    for s in range(nslab):
        xp_ref[s, pad:, :] = xb_ref[0, :, lanes(s)]
    step = {m: [xp_ref[s, pl.ds(pad + m, M, stride=SUBLANES), :] for s in range(nslab)]
            for m in range(-hist, SUBLANES)}
    conv = []
    for k in range(SUBLANES):
        cols = [cbias_ref[:, lanes(s)]
                + sum(step[k - hist + i][s] * cw_ref[i:i + 1, lanes(s)] for i in range(CONV_W))
                for s in range(nslab)]
        conv.append(jnp.concatenate(cols, axis=-1))
    cn_ref[0] = jnp.concatenate([xp_ref[s, pad + tt - hist:pad + tt, :] for s in range(nslab)], axis=-1)
    for s in range(nslab):
        xp_ref[s, 0:pad, :] = xp_ref[s, tt:tt + pad, :]

    a, u = _lru_coeffs(jnp.concatenate(conv, axis=0), wr_ref, br_ref, wi_ref, bi_ref, lam_ref, nblk)

    a_run, u_run = a[0:M], u[0:M]
    a_steps, u_steps = [a_run], [u_run]
    for k in range(1, SUBLANES):
        a_k = a[k * M:(k + 1) * M]
        u_run = a_k * u_run + u[k * M:(k + 1) * M]
        a_run = a_k * a_run
        a_steps.append(a_run)
        u_steps.append(u_run)

    car_ref[0] = a_run
    car_ref[1] = u_run
    hin_ref[0:1, :] = h_ref[0]

    def body(g, h):
        h = car_ref[0, pl.ds(g, 1), :] * h + car_ref[1, pl.ds(g, 1), :]
        hin_ref[pl.ds(g + 1, 1), :] = h
        return h

    h_last = lax.fori_loop(0, M, body, h_ref[0])
    h_ref[0] = h_last
    ht_ref[0] = h_last
    h_in = hin_ref[0:M, :]
    for k in range(SUBLANES):
        h_k = a_steps[k] * h_in + u_steps[k]
        for s in range(nslab):
            hs_ref[s, pl.ds(k, M, stride=SUBLANES), :] = h_k[:, lanes(s)]
    y = jnp.concatenate([hs_ref[s] for s in range(nslab)], axis=-1)
    yz_ref[0] = (y * _silu(zg_ref[0])).astype(yz_ref.dtype)


def _lru_kernel(xb_ref, zg_ref, conv0_ref, h0_ref, cw_ref, cbias_ref, wr_ref, br_ref, wi_ref, bi_ref,
                lam_ref, yz_ref, ht_ref, cn_ref, xp_ref, a_ref, u_ref, h_ref, *, bb, tt, nblk):
    pad = SUBLANES
    hist = CONV_W - 1

    @pl.when(pl.program_id(2) == 0)
    def _():
        xp_ref[:, pad - hist:pad, :] = conv0_ref[...]
        h_ref[...] = h0_ref[...]

    xp_ref[:, pad:, :] = xb_ref[...]
    xc = cbias_ref[...] + sum(
        xp_ref[:, pad - hist + i:pad - hist + i + tt, :] * cw_ref[i:i + 1, :] for i in range(CONV_W))
    cn_ref[...] = xp_ref[:, pad + tt - hist:pad + tt, :]
    xp_ref[:, 0:pad, :] = xp_ref[:, tt:tt + pad, :]

    width = nblk * GATE_BLOCK
    a, u = _lru_coeffs(xc.reshape(bb * tt, width), wr_ref, br_ref, wi_ref, bi_ref, lam_ref, nblk)

    nslab = width // LANES
    for s in range(nslab):
        a_ref[s] = a[:, s * LANES:(s + 1) * LANES]
        u_ref[s] = u[:, s * LANES:(s + 1) * LANES]

    ngroups = bb * tt // SUBLANES

    def step_rows(k):
        return pl.ds(k, ngroups, stride=SUBLANES)

    a_run = a_ref[:, step_rows(0), :]
    u_run = u_ref[:, step_rows(0), :]
    for k in range(1, SUBLANES):
        a_k = a_ref[:, step_rows(k), :]
        u_run = a_k * u_run + u_ref[:, step_rows(k), :]
        a_run = a_k * a_run
        a_ref[:, step_rows(k), :] = a_run
        u_ref[:, step_rows(k), :] = u_run

    h_in = jnp.stack([h_ref[:, :, s * LANES:(s + 1) * LANES] for s in range(nslab)], axis=0)
    if bb == 1:
        def body(gidx, h):
            s = pl.multiple_of(gidx * SUBLANES, SUBLANES)
            hg = a_ref[:, pl.ds(s, SUBLANES), :] * h + u_ref[:, pl.ds(s, SUBLANES), :]
            u_ref[:, pl.ds(s, SUBLANES), :] = hg
            return hg[:, SUBLANES - 1:SUBLANES, :]

        h_out = lax.fori_loop(0, tt // SUBLANES, body, h_in.reshape(nslab, 1, LANES))
        h_out = h_out.reshape(nslab, 1, 1, LANES)
        hs = u_ref[...]
    else:
        a4 = a_ref[...].reshape(nslab, bb, tt, LANES)
        u4 = u_ref[...].reshape(nslab, bb, tt, LANES)
        h_out, parts = h_in, []
        for gidx in range(tt // SUBLANES):
            sl = slice(gidx * SUBLANES, (gidx + 1) * SUBLANES)
            hg = a4[:, :, sl] * h_out + u4[:, :, sl]
            parts.append(hg)
            h_out = hg[:, :, SUBLANES - 1:SUBLANES]
        hs = jnp.concatenate(parts, axis=2).reshape(nslab, bb * tt, LANES)

    h_last = jnp.concatenate([h_out[s] for s in range(nslab)], axis=-1)
    h_ref[...] = h_last
    ht_ref[...] = h_last
    y = jnp.concatenate([hs[s] for s in range(nslab)], axis=-1).reshape(bb, tt, width)
    yz_ref[...] = (y * _silu(zg_ref[...])).astype(yz_ref.dtype)


def _lru_call(xb, zg, conv0, h0, conv_w, conv_b, w_r, b_r, w_i, b_i, lam, *, bb, tt, nblk):
    B, T, Dr = xb.shape
    C = nblk * GATE_BLOCK
    hist = CONV_W - 1
    big = pl.BlockSpec((bb, tt, C), lambda b, c, t: (b, t, c))
    vec = pl.BlockSpec((1, C), lambda b, c, t: (0, c))
    gate_w = pl.BlockSpec((nblk, GATE_BLOCK, GATE_BLOCK), lambda b, c, t: (c, 0, 0))
    state3 = pl.BlockSpec((bb, hist, C), lambda b, c, t: (b, 0, c))
    state1 = pl.BlockSpec((bb, 1, C), lambda b, c, t: (b, 0, c))
    nslab = C // LANES
    if bb == 1:
        body = functools.partial(_lru_steps_kernel, tt=tt, nblk=nblk)
        scratch = [pltpu.VMEM((nslab, tt + SUBLANES, LANES), F32),
                   pltpu.VMEM((nslab, tt, LANES), F32),
                   pltpu.VMEM((2, tt // SUBLANES, C), F32),
                   pltpu.VMEM((tt // SUBLANES + SUBLANES, C), F32),
                   pltpu.VMEM((1, 1, C), F32)]
    else:
        body = functools.partial(_lru_kernel, bb=bb, tt=tt, nblk=nblk)
        scratch = [pltpu.VMEM((bb, tt + SUBLANES, C), F32),
                   pltpu.VMEM((nslab, bb * tt, LANES), F32),
                   pltpu.VMEM((nslab, bb * tt, LANES), F32),
                   pltpu.VMEM((bb, 1, C), F32)]
    return pl.pallas_call(
        body,
        grid=(B // bb, Dr // C, T // tt),
        in_specs=[big, big, state3, state1,
                  pl.BlockSpec((CONV_W, C), lambda b, c, t: (0, c)), vec,
                  gate_w, vec, gate_w, vec, vec],
        out_specs=[big, state1, state3],
        out_shape=[jax.ShapeDtypeStruct((B, T, Dr), BF16),
                   jax.ShapeDtypeStruct((B, 1, Dr), F32),
                   jax.ShapeDtypeStruct((B, hist, Dr), F32)],
        scratch_shapes=scratch,
        compiler_params=_cparams(("parallel", "parallel", "arbitrary")),
        name="rglru",
    )(xb, zg, conv0, h0.reshape(B, 1, Dr), conv_w, conv_b.reshape(1, Dr), w_r, b_r.reshape(1, Dr),
      w_i, b_i.reshape(1, Dr), lam.reshape(1, Dr))


def _mpr_kernel(a_ref, w_ref, x_ref, gate_ref, g_ref, o_ref, out_ref, *, tn):
    a = a_ref[...]
    for c in range(out_ref.shape[0]):
        out_ref[c] = _dot(a, w_ref[:, c * tn:(c + 1) * tn])
    out = jnp.concatenate([out_ref[c] for c in range(out_ref.shape[0])], axis=1)
    n = out * lax.rsqrt(jnp.mean(out * out, axis=-1, keepdims=True) + EPS) * g_ref[...]
    o_ref[...] = x_ref[...] + gate_ref[...] * n.reshape(o_ref.shape)


def _mpr_call(a, w, x, gate, g, *, bb, tt, tn, name):
    B, T, D = x.shape
    K = a.shape[1]
    nt = T // tt
    tm = bb * tt
    xspec = pl.BlockSpec((bb, tt, D), lambda i: (i // nt, i % nt, 0))
    return pl.pallas_call(
        functools.partial(_mpr_kernel, tn=tn),
        grid=((B // bb) * nt,),
        in_specs=[
            pl.BlockSpec((tm, K), lambda i: (i, 0)),
            pl.BlockSpec((K, D), lambda i: (0, 0), pipeline_mode=pl.Buffered(1)),
            xspec,
            pl.BlockSpec((bb, 1, D), lambda i: (i // nt, 0, 0)),
            pl.BlockSpec((1, D), lambda i: (0, 0)),
        ],
        out_specs=xspec,
        out_shape=jax.ShapeDtypeStruct((B, T, D), F32),
        scratch_shapes=[pltpu.VMEM((D // tn, tm, tn), F32)],
        compiler_params=_cparams(("parallel",)),
        name=name,
    )(a, w, x, gate, g.reshape(1, D))


def _rev_tri(n):
    j = lax.broadcasted_iota(jnp.int32, (n, n), 0)
    s = lax.broadcasted_iota(jnp.int32, (n, n), 1)
    return jnp.where(j >= s, 1.0, 0.0).astype(BF16)


def _sb_tile(q, kb, vb, tri, carry, mask):
    z2 = lax.dot_general(q, kb, (((1,), (1,)), ((), ())), preferred_element_type=F32)
    sp = _softplus2(z2)
    if mask is not None:
        sp = jnp.where(mask, sp, 0.0)
    hi = sp.astype(BF16)
    lo = (sp - hi.astype(F32)).astype(BF16)
    csum = _dot(hi, tri) + _dot(lo, tri)
    p = jnp.exp2(z2 - csum - carry)
    if mask is not None:
        p = jnp.where(mask, p, 0.0)
    return _dot(p.astype(BF16), vb), csum[:, 0:1]


def _attn_self_kernel(q_ref, zg_ref, k_ref, v_ref, o_ref, acc_ref, carry_ref,
                      z_ref, hi_ref, lo_ref, *, tq, tk, nh):
    i = pl.program_id(2)
    q_all = (q_ref[0] * QSCALE).astype(BF16)
    qs = [q_all[:, g * HEAD_DIM:(g + 1) * HEAD_DIM] for g in range(nh)]
    tri = _rev_tri(tk)
    nd = tq // tk
    n = i * nd

    def keys(ref, g, kt):
        ks = pl.multiple_of(kt * tk, tk)
        return ref[0, pl.ds(ks, tk), g * HEAD_DIM:(g + 1) * HEAD_DIM]

    def tile(g, kt, r0, mask):
        pv, tot = _sb_tile(qs[g][r0:], keys(k_ref, g, kt), keys(v_ref, g, kt), tri,
                           carry_ref[g, r0:, :], mask)
        acc_ref[g, r0:, :] += pv
        carry_ref[g, r0:, :] += tot

    def scores(g, kt, slot):
        z2 = lax.dot_general(qs[g], keys(k_ref, g, kt), (((1,), (1,)), ((), ())),
                             preferred_element_type=F32)
        sp = _softplus2(z2)
        hi = sp.astype(BF16)
        z_ref[g, slot] = z2
        hi_ref[g, slot] = hi
        lo_ref[g, slot] = (sp - hi.astype(F32)).astype(BF16)

    def weights(g, kt, slot):
        csum = _dot(hi_ref[g, slot], tri) + _dot(lo_ref[g, slot], tri)
        p = jnp.exp2(z_ref[g, slot] - csum - carry_ref[g])
        acc_ref[g] += _dot(p.astype(BF16), keys(v_ref, g, kt))
        carry_ref[g] += csum[:, 0:1]

    acc_ref[...] = jnp.zeros_like(acc_ref)
    carry_ref[...] = jnp.zeros_like(carry_ref)
    for g in range(nh):
        scores(g, jnp.maximum(n - 1, 0), 0)
    for c in reversed(range(nd)):
        rows = tq - c * tk
        row = lax.broadcasted_iota(jnp.int32, (rows, tk), 0)
        col = lax.broadcasted_iota(jnp.int32, (rows, tk), 1)
        for g in range(nh):
            tile(g, n + c, c * tk, col < row)

    def body(m, c):
        t = 2 * m
        for g in range(nh):
            scores(g, n - 2 - t, 1)
            weights(g, n - 1 - t, 0)
        for g in range(nh):
            scores(g, jnp.maximum(n - 3 - t, 0), 0)
            weights(g, n - 2 - t, 1)
        return c

    lax.fori_loop(0, n // 2, body, 0)
    acc = jnp.concatenate([acc_ref[g] for g in range(nh)], axis=1)
    o_ref[0] = (acc * _silu(zg_ref[0])).astype(o_ref.dtype)


def _attn_self_call(q, zg, k, v, *, tq, tk, nh):
    B, T, _ = q.shape
    assert (tq // tk) % 2 == 0 and tq % tk == 0 and T % tq == 0
    wide = nh * HEAD_DIM
    qspec = pl.BlockSpec((1, tq, wide), lambda b, h, i: (b, i, h))
    kspec = pl.BlockSpec((1, T, wide), lambda b, h, i: (b, 0, h))
    return pl.pallas_call(
        functools.partial(_attn_self_kernel, tq=tq, tk=tk, nh=nh),
        grid=(B, N_HEADS // nh, T // tq),
        in_specs=[qspec, qspec, kspec, kspec],
        out_specs=qspec,
        out_shape=jax.ShapeDtypeStruct((B, T, D_ATT), BF16),
        scratch_shapes=[pltpu.VMEM((nh, tq, HEAD_DIM), F32), pltpu.VMEM((nh, tq, 1), F32),
                        pltpu.VMEM((nh, 2, tq, tk), F32), pltpu.VMEM((nh, 2, tq, tk), BF16),
                        pltpu.VMEM((nh, 2, tq, tk), BF16)],
        compiler_params=_cparams(("parallel", "parallel", "arbitrary")),
        name="sb_attn_self",
    )(q, zg, k, v)


def _attn_past_kernel(q_ref, zg_ref, kn_ref, vn_ref, kc_ref, vc_ref, o_ref, acc_ref, carry_ref, *, T, tp, blk):
    j = pl.program_id(2)
    nh = SUBLANES
    nc = tp // blk
    q_all = (q_ref[0] * QSCALE).astype(BF16)

    @pl.when(j == 0)
    def _():
        lane = lax.broadcasted_iota(jnp.int32, q_all.shape, 1)
        q_bd = jnp.concatenate(
            [jnp.where((lane >= h * HEAD_DIM) & (lane < (h + 1) * HEAD_DIM), q_all, jnp.zeros_like(q_all))
             for h in range(nh)], axis=0)
        t_idx = jnp.concatenate([lax.broadcasted_iota(jnp.int32, (T, T), 0)] * nh, axis=0)
        s_idx = lax.broadcasted_iota(jnp.int32, (nh * T, T), 1)
        pv, tot = _sb_tile(q_bd, kn_ref[0].astype(BF16), vn_ref[0].astype(BF16), _rev_tri(T),
                           jnp.zeros((nh * T, 1), F32), s_idx < t_idx)
        for h in range(nh):
            acc_ref[h] = pv[h * T:(h + 1) * T, h * HEAD_DIM:(h + 1) * HEAD_DIM]
            carry_ref[h] = tot[h * T:(h + 1) * T]

    kc2 = kc_ref.reshape(tp * nh, HEAD_DIM)
    vc2 = vc_ref.reshape(tp * nh, HEAD_DIM)
    zs, lks = [], []
    for h in range(nh):
        kh = kc2[pl.ds(h, tp, stride=nh), :].astype(BF16)
        z = lax.dot_general(q_all[:, h * HEAD_DIM:(h + 1) * HEAD_DIM], kh, (((1,), (1,)), ((), ())),
                            preferred_element_type=F32)
        zs.append(z)
        lks.append(_softplus2(z))
    stack = jnp.concatenate([lk[:, c * blk:(c + 1) * blk] for lk in lks for c in range(nc)], axis=0)
    hi = stack.astype(BF16)
    lo = (stack - hi.astype(F32)).astype(BF16)
    tri = _rev_tri(blk)
    cs = _dot(hi, tri) + _dot(lo, tri)
    for h in range(nh):
        carry = carry_ref[h]
        ps = [None] * nc
        for c in reversed(range(nc)):
            r0 = (h * nc + c) * T
            csum = cs[r0:r0 + T]
            ps[c] = jnp.exp2(zs[h][:, c * blk:(c + 1) * blk] - csum - carry)
            carry = carry + csum[:, 0:1]
        p = jnp.concatenate(ps, axis=1).astype(BF16)
        vh = vc2[pl.ds(h, tp, stride=nh), :].astype(BF16)
        acc_ref[h] += _dot(p, vh)
        carry_ref[h] = carry

    @pl.when(j == pl.num_programs(2) - 1)
    def _():
        acc = jnp.concatenate([acc_ref[h] for h in range(nh)], axis=1)
        o_ref[0] = (acc * _silu(zg_ref[0])).astype(o_ref.dtype)


def _attn_past_call(q, zg, k_new, v_new, k_past, v_past, *, tp, blk):
    B, T, _ = q.shape
    P = k_past.shape[1]
    nh = SUBLANES
    ng = N_HEADS // nh
    nt = P // tp
    wide = nh * HEAD_DIM
    qspec = pl.BlockSpec((1, T, wide), lambda b, g, j: (b, 0, g))
    pspec = pl.BlockSpec((None, tp, None, nh, HEAD_DIM), lambda b, g, j: (b, nt - 1 - j, g, 0, 0))
    k5 = k_past.reshape(B, P, ng, nh, HEAD_DIM)
    v5 = v_past.reshape(B, P, ng, nh, HEAD_DIM)
    return pl.pallas_call(
        functools.partial(_attn_past_kernel, T=T, tp=tp, blk=blk),
        grid=(B, ng, nt),
        in_specs=[qspec, qspec, qspec, qspec, pspec, pspec],
        out_specs=qspec,
        out_shape=jax.ShapeDtypeStruct((B, T, D_ATT), BF16),
        scratch_shapes=[pltpu.VMEM((nh, T, HEAD_DIM), F32), pltpu.VMEM((nh, T, 1), F32)],
        compiler_params=_cparams(("parallel", "parallel", "arbitrary")),
        name="sb_attn_past",
    )(q, zg, k_new, v_new, k5, v5)


class _Plan(NamedTuple):
    in_rows: tuple
    out_rows: tuple
    lru_rows: tuple
    cols: int
    lru_blocks: int
    tq: int
    heads: int
    tp: int


def _plan(B, T):
    long_seq = T >= 4 * MXU_DIM
    return _Plan(
        in_rows=(1, 4 * MXU_DIM) if long_seq else (B, T),
        out_rows=(1, 2 * MXU_DIM) if long_seq else (B, T),
        lru_rows=(1, 2 * MXU_DIM) if long_seq else (B, T),
        cols=(2 if long_seq else 4) * MXU_DIM, lru_blocks=4, tq=2 * MXU_DIM, heads=4, tp=8 * MXU_DIM)


def _run_group(x, mods, h0, conv0, past_k, past_v, p):
    B, T, D = x.shape
    plan = _plan(B, T)
    in_bb, in_tt = plan.in_rows
    out_bb, out_tt = plan.out_rows
    lru_bb, lru_tt = plan.lru_rows
    hs, convs = [], []
    k_new = v_new = None
    for l in range(DEPTH):
        shift, scale, gate = mods[l]
        if l < N_A:
            xb, zg = _nmm_call(x, p["g_pre"][l], scale, shift, p["w_in_a"][l],
                               bb=in_bb, tt=in_tt, tn=plan.cols, name="in_proj_a")
            yz, ht, cn = _lru_call(xb.reshape(B, T, D_RNN), zg.reshape(B, T, D_RNN), conv0[l], h0[l],
                                   p["conv_w"][l], p["conv_b"][l], p["w_rgate"][l], p["b_rgate"][l],
                                   p["w_igate"][l], p["b_igate"][l], p["lru_lambda"][l],
                                   bb=lru_bb, tt=lru_tt, nblk=plan.lru_blocks)
            hs.append(ht.reshape(B, D_RNN))
            convs.append(cn)
            x = _mpr_call(yz.reshape(B * T, D_RNN), p["w_out_a"][l], x, gate, p["g_post"][l],
                          bb=out_bb, tt=out_tt, tn=plan.cols, name="out_proj_a")
        else:
            if l == N_A:
                zero = jnp.zeros_like(scale)
                k_new, v_new, k16, v16 = _nmm_call(x, p["g_kv"], zero, zero, p["w_kv"], bb=in_bb,
                                                   tt=in_tt, tn=plan.cols, name="kv_proj",
                                                   bf16_copies=True)
            j = l - N_A
            q, zg = _nmm_call(x, p["g_pre"][l], scale, shift, p["w_in_b"][j],
                              bb=in_bb, tt=in_tt, tn=plan.cols, name="in_proj_b")
            q3, zg3 = q.reshape(B, T, D_ATT), zg.reshape(B, T, D_ATT)
            k3, v3 = k16.reshape(B, T, D_ATT), v16.reshape(B, T, D_ATT)
            if past_k is None:
                oz = _attn_self_call(q3, zg3, k3, v3, tq=plan.tq, tk=MXU_DIM, nh=plan.heads)
            else:
                oz = _attn_past_call(q3, zg3, k3, v3, past_k, past_v, tp=plan.tp, blk=MXU_DIM)
            x = _mpr_call(oz.reshape(B * T, D_ATT), p["w_out_b"][j], x, gate, p["g_post"][l],
                          bb=out_bb, tt=out_tt, tn=plan.cols, name="out_proj_b")
    return (x, k_new.reshape(B, T, N_HEADS, HEAD_DIM), v_new.reshape(B, T, N_HEADS, HEAD_DIM),
            jnp.stack(hs), jnp.stack(convs))


def kernel(x_prompt, x_sample, c_prompt, c_sample, cache_k, cache_v, state_lru, state_conv, g_pre, g_post, w_ada, b_ada, w_in_a, conv_w, conv_b, w_rgate, b_rgate, w_igate, b_igate, lru_lambda, w_out_a, g_kv, w_kv, w_in_b, w_out_b):
    Bp, Bs = x_prompt.shape[0], x_sample.shape[0]
    p = dict(g_pre=g_pre, g_post=g_post, conv_w=conv_w, conv_b=conv_b, b_rgate=b_rgate, b_igate=b_igate,
             lru_lambda=lru_lambda, g_kv=g_kv,
             w_in_a=w_in_a.astype(BF16), w_rgate=w_rgate.astype(BF16), w_igate=w_igate.astype(BF16),
             w_out_a=w_out_a.astype(BF16), w_kv=w_kv.astype(BF16), w_in_b=w_in_b.astype(BF16),
             w_out_b=w_out_b.astype(BF16))

    rows = Bp + Bs
    rows_pad = -(-rows // SUBLANES) * SUBLANES
    c_all = jnp.concatenate([c_prompt, c_sample, jnp.zeros((rows_pad - rows, D_MODEL), F32)], axis=0)
    m = _ada_call(c_all, w_ada, b_ada)

    def mods_of(lo, hi):
        out = []
        for l in range(DEPTH):
            ml = m[l, lo:hi]
            out.append(tuple(ml[:, i * D_MODEL:(i + 1) * D_MODEL][:, None, :] for i in range(3)))
        return out

    h0_p = jnp.zeros((N_A, Bp, D_RNN), F32)
    conv0_p = jnp.zeros((N_A, Bp, CONV_W - 1, D_RNN), F32)
    y_p, k_p, v_p, lru_p, conv_p = _run_group(x_prompt, mods_of(0, Bp), h0_p, conv0_p, None, None, p)
    y_s, k_s, v_s, lru_s, conv_s = _run_group(x_sample, mods_of(Bp, rows), state_lru, state_conv,
                                              cache_k, cache_v, p)
    return (y_p, y_s, k_p, v_p, k_s, v_s, lru_p, lru_s, conv_p, conv_s)
```

```python
import functools
from typing import NamedTuple

import jax
import jax.numpy as jnp
from jax import lax
from jax.experimental import pallas as pl
from jax.experimental.pallas import tpu as pltpu

D_MODEL = 2048
DEPTH = 2
N_A = DEPTH // 2
N_B = DEPTH - N_A
D_RNN = 2 * D_MODEL
N_GATE_BLOCKS = 16
GATE_BLOCK = D_RNN // N_GATE_BLOCKS
CONV_W = 4
LRU_C = 8.0
N_HEADS = 16
HEAD_DIM = D_MODEL // N_HEADS
D_ATT = N_HEADS * HEAD_DIM
EPS = 1e-6
LOG2E = 1.4426950408889634
QSCALE = HEAD_DIM ** -0.5 * LOG2E

SUBLANES = 8
LANES = 128
MXU_DIM = 256
VMEM_LIMIT = 56 * 1024 * 1024

F32 = jnp.float32
BF16 = jnp.bfloat16


def _cparams(sem):
    return pltpu.CompilerParams(dimension_semantics=sem, vmem_limit_bytes=VMEM_LIMIT)


def _dot(a, b):
    return jnp.dot(a, b, preferred_element_type=F32)


def _sigmoid(x):
    return 0.5 * jnp.tanh(0.5 * x) + 0.5


def _silu(x):
    h = 0.5 * x
    return h + h * jnp.tanh(h)


def _softplus(x):
    return jnp.maximum(x, 0.0) + jnp.log(1.0 + jnp.exp(-jnp.abs(x)))


def _softplus2(x2):
    return jnp.maximum(x2, 0.0) + jnp.log(1.0 + jnp.exp2(-jnp.abs(x2))) * LOG2E


def _ada_kernel(c_ref, w_ref, b_ref, o_ref):
    c = c_ref[...]
    s = _silu(c).astype(BF16)
    o_ref[...] = _dot(s, w_ref[...].astype(BF16)) + b_ref[...]


def _ada_call(c_all, w_ada, b_ada):
    rows = c_all.shape[0]
    tn = 3 * MXU_DIM
    nout = 3 * D_MODEL
    return pl.pallas_call(
        _ada_kernel,
        grid=(DEPTH, nout // tn),
        in_specs=[
            pl.BlockSpec((rows, D_MODEL), lambda l, j: (0, 0)),
            pl.BlockSpec((None, D_MODEL, tn), lambda l, j: (l, 0, j)),
            pl.BlockSpec((None, 1, tn), lambda l, j: (l, 0, j)),
        ],
        out_specs=pl.BlockSpec((None, rows, tn), lambda l, j: (l, 0, j)),
        out_shape=jax.ShapeDtypeStruct((DEPTH, rows, nout), F32),
        compiler_params=_cparams(("parallel", "parallel")),
        name="ada_mod",
    )(c_all, w_ada, b_ada.reshape(DEPTH, 1, nout))


def _nmm_kernel(x_ref, g_ref, sc_ref, sh_ref, wl_ref, wr_ref, *rest):
    *out_refs, hn_ref = rest

    @pl.when(pl.program_id(1) == 0)
    def _():
        x = x_ref[...]
        y = x * lax.rsqrt(jnp.mean(x * x, axis=-1, keepdims=True) + EPS)
        hn = y * (g_ref[...] * (1.0 + sc_ref[...])) + sh_ref[...]
        hn_ref[...] = hn.reshape(hn_ref.shape).astype(BF16)

    hn = hn_ref[...]
    left = _dot(hn, wl_ref[...])
    right = _dot(hn, wr_ref[...])
    for o_ref, val in zip(out_refs, (left, right, left, right)):
        o_ref[...] = val.astype(o_ref.dtype)


def _nmm_call(x, g, scale, shift, w, *, bb, tt, tn, name, bf16_copies=False):
    B, T, D = x.shape
    nh = w.shape[1] // 2
    nt = T // tt
    nj = nh // tn
    tm = bb * tt
    out = [jax.ShapeDtypeStruct((B * T, nh), F32)] * 2
    if bf16_copies:
        out = out + [jax.ShapeDtypeStruct((B * T, nh), BF16)] * 2
    return pl.pallas_call(
        _nmm_kernel,
        grid=((B // bb) * nt, nj),
        in_specs=[
            pl.BlockSpec((bb, tt, D), lambda i, j: (i // nt, i % nt, 0)),
            pl.BlockSpec((1, 1, D), lambda i, j: (0, 0, 0)),
            pl.BlockSpec((bb, 1, D), lambda i, j: (i // nt, 0, 0)),
            pl.BlockSpec((bb, 1, D), lambda i, j: (i // nt, 0, 0)),
            pl.BlockSpec((D, tn), lambda i, j: (0, j)),
            pl.BlockSpec((D, tn), lambda i, j: (0, j + nj)),
        ],
        out_specs=[pl.BlockSpec((tm, tn), lambda i, j: (i, j))] * len(out),
        out_shape=out,
        scratch_shapes=[pltpu.VMEM((tm, D), BF16)],
        compiler_params=_cparams(("parallel", "arbitrary")),
        name=name,
    )(x, g.reshape(1, 1, D), scale, shift, w, w)


def _lru_coeffs(x2, wr_ref, br_ref, wi_ref, bi_ref, lam_ref, nblk):
    x16 = x2.astype(BF16)
    rg, ig = [], []
    for k in range(nblk):
        xk = x16[:, k * GATE_BLOCK:(k + 1) * GATE_BLOCK]
        rg.append(_dot(xk, wr_ref[k]))
        ig.append(_dot(xk, wi_ref[k]))
    ig = _sigmoid(jnp.concatenate(ig, axis=1) + bi_ref[...])
    half = (-0.5 * LRU_C) * _softplus(-lam_ref[...])
    log_a = jnp.tanh(0.5 * (jnp.concatenate(rg, axis=1) + br_ref[...])) * half + half
    a = jnp.exp(log_a)
    w = -jnp.tanh(log_a) * (a * a + 1.0)
    u = jnp.exp2((0.5 * LOG2E) * jnp.log(w)) * (ig * x2)
    return a, u


def _lru_steps_kernel(xb_ref, zg_ref, conv0_ref, h0_ref, cw_ref, cbias_ref, wr_ref, br_ref, wi_ref,
                      bi_ref, lam_ref, yz_ref, ht_ref, cn_ref, xp_ref, hs_ref, car_ref, hin_ref, h_ref,
                      *, tt, nblk):
    pad = SUBLANES
    hist = CONV_W - 1
    width = nblk * GATE_BLOCK
    nslab = width // LANES
    M = tt // SUBLANES

    def lanes(s):
        return slice(s * LANES, (s + 1) * LANES)

    @pl.when(pl.program_id(2) == 0)
    def _():
        for s in range(nslab):
            xp_ref[s, pad - hist:pad, :] = conv0_ref[0, :, lanes(s)]
        h_ref[...] = h0_ref[...]

    for s in range(nslab):
        xp_ref[s, pad:, :] = xb_ref[0, :, lanes(s)]
    step = {m: [xp_ref[s, pl.ds(pad + m, M, stride=SUBLANES), :] for s in range(nslab)]
            for m in range(-hist, SUBLANES)}
    conv = []
    for k in range(SUBLANES):
        cols = [cbias_ref[:, lanes(s)]
                + sum(step[k - hist + i][s] * cw_ref[i:i + 1, lanes(s)] for i in range(CONV_W))
                for s in range(nslab)]
        conv.append(jnp.concatenate(cols, axis=-1))
    cn_ref[0] = jnp.concatenate([xp_ref[s, pad + tt - hist:pad + tt, :] for s in range(nslab)], axis=-1)
    for s in range(nslab):
        xp_ref[s, 0:pad, :] = xp_ref[s, tt:tt + pad, :]

    a, u = _lru_coeffs(jnp.concatenate(conv, axis=0), wr_ref, br_ref, wi_ref, bi_ref, lam_ref, nblk)

    a_run, u_run = a[0:M], u[0:M]
    a_steps, u_steps = [a_run], [u_run]
    for k in range(1, SUBLANES):
        a_k = a[k * M:(k + 1) * M]
        u_run = a_k * u_run + u[k * M:(k + 1) * M]
        a_run = a_k * a_run
        a_steps.append(a_run)
        u_steps.append(u_run)

    car_ref[0] = a_run
    car_ref[1] = u_run
    hin_ref[0:1, :] = h_ref[0]

    def body(g, h):
        h = car_ref[0, pl.ds(g, 1), :] * h + car_ref[1, pl.ds(g, 1), :]
        hin_ref[pl.ds(g + 1, 1), :] = h
        return h

    h_last = lax.fori_loop(0, M, body, h_ref[0])
    h_ref[0] = h_last
    ht_ref[0] = h_last
    h_in = hin_ref[0:M, :]
    for k in range(SUBLANES):
        h_k = a_steps[k] * h_in + u_steps[k]
        for s in range(nslab):
            hs_ref[s, pl.ds(k, M, stride=SUBLANES), :] = h_k[:, lanes(s)]
    y = jnp.concatenate([hs_ref[s] for s in range(nslab)], axis=-1)
    yz_ref[0] = (y * _silu(zg_ref[0])).astype(yz_ref.dtype)


def _lru_kernel(xb_ref, zg_ref, conv0_ref, h0_ref, cw_ref, cbias_ref, wr_ref, br_ref, wi_ref, bi_ref,
                lam_ref, yz_ref, ht_ref, cn_ref, xp_ref, a_ref, u_ref, h_ref, *, bb, tt, nblk):
    pad = SUBLANES
    hist = CONV_W - 1

    @pl.when(pl.program_id(2) == 0)
    def _():
        xp_ref[:, pad - hist:pad, :] = conv0_ref[...]
        h_ref[...] = h0_ref[...]

    xp_ref[:, pad:, :] = xb_ref[...]
    xc = cbias_ref[...] + sum(
        xp_ref[:, pad - hist + i:pad - hist + i + tt, :] * cw_ref[i:i + 1, :] for i in range(CONV_W))
    cn_ref[...] = xp_ref[:, pad + tt - hist:pad + tt, :]
    xp_ref[:, 0:pad, :] = xp_ref[:, tt:tt + pad, :]

    width = nblk * GATE_BLOCK
    a, u = _lru_coeffs(xc.reshape(bb * tt, width), wr_ref, br_ref, wi_ref, bi_ref, lam_ref, nblk)

    nslab = width // LANES
    for s in range(nslab):
        a_ref[s] = a[:, s * LANES:(s + 1) * LANES]
        u_ref[s] = u[:, s * LANES:(s + 1) * LANES]

    ngroups = bb * tt // SUBLANES

    def step_rows(k):
        return pl.ds(k, ngroups, stride=SUBLANES)

    a_run = a_ref[:, step_rows(0), :]
    u_run = u_ref[:, step_rows(0), :]
    for k in range(1, SUBLANES):
        a_k = a_ref[:, step_rows(k), :]
        u_run = a_k * u_run + u_ref[:, step_rows(k), :]
        a_run = a_k * a_run
        a_ref[:, step_rows(k), :] = a_run
        u_ref[:, step_rows(k), :] = u_run

    h_in = jnp.stack([h_ref[:, :, s * LANES:(s + 1) * LANES] for s in range(nslab)], axis=0)
    if bb == 1:
        def body(gidx, h):
            s = pl.multiple_of(gidx * SUBLANES, SUBLANES)
            hg = a_ref[:, pl.ds(s, SUBLANES), :] * h + u_ref[:, pl.ds(s, SUBLANES), :]
            u_ref[:, pl.ds(s, SUBLANES), :] = hg
            return hg[:, SUBLANES - 1:SUBLANES, :]

        h_out = lax.fori_loop(0, tt // SUBLANES, body, h_in.reshape(nslab, 1, LANES))
        h_out = h_out.reshape(nslab, 1, 1, LANES)
        hs = u_ref[...]
    else:
        a4 = a_ref[...].reshape(nslab, bb, tt, LANES)
        u4 = u_ref[...].reshape(nslab, bb, tt, LANES)
        h_out, parts = h_in, []
        for gidx in range(tt // SUBLANES):
            sl = slice(gidx * SUBLANES, (gidx + 1) * SUBLANES)
            hg = a4[:, :, sl] * h_out + u4[:, :, sl]
            parts.append(hg)
            h_out = hg[:, :, SUBLANES - 1:SUBLANES]
        hs = jnp.concatenate(parts, axis=2).reshape(nslab, bb * tt, LANES)

    h_last = jnp.concatenate([h_out[s] for s in range(nslab)], axis=-1)
    h_ref[...] = h_last
    ht_ref[...] = h_last
    y = jnp.concatenate([hs[s] for s in range(nslab)], axis=-1).reshape(bb, tt, width)
    yz_ref[...] = (y * _silu(zg_ref[...])).astype(yz_ref.dtype)


def _lru_call(xb, zg, conv0, h0, conv_w, conv_b, w_r, b_r, w_i, b_i, lam, *, bb, tt, nblk):
    B, T, Dr = xb.shape
    C = nblk * GATE_BLOCK
    hist = CONV_W - 1
    big = pl.BlockSpec((bb, tt, C), lambda b, c, t: (b, t, c))
    vec = pl.BlockSpec((1, C), lambda b, c, t: (0, c))
    gate_w = pl.BlockSpec((nblk, GATE_BLOCK, GATE_BLOCK), lambda b, c, t: (c, 0, 0))
    state3 = pl.BlockSpec((bb, hist, C), lambda b, c, t: (b, 0, c))
    state1 = pl.BlockSpec((bb, 1, C), lambda b, c, t: (b, 0, c))
    nslab = C // LANES
    if bb == 1:
        body = functools.partial(_lru_steps_kernel, tt=tt, nblk=nblk)
        scratch = [pltpu.VMEM((nslab, tt + SUBLANES, LANES), F32),
                   pltpu.VMEM((nslab, tt, LANES), F32),
                   pltpu.VMEM((2, tt // SUBLANES, C), F32),
                   pltpu.VMEM((tt // SUBLANES + SUBLANES, C), F32),
                   pltpu.VMEM((1, 1, C), F32)]
    else:
        body = functools.partial(_lru_kernel, bb=bb, tt=tt, nblk=nblk)
        scratch = [pltpu.VMEM((bb, tt + SUBLANES, C), F32),
                   pltpu.VMEM((nslab, bb * tt, LANES), F32),
                   pltpu.VMEM((nslab, bb * tt, LANES), F32),
                   pltpu.VMEM((bb, 1, C), F32)]
    return pl.pallas_call(
        body,
        grid=(B // bb, Dr // C, T // tt),
        in_specs=[big, big, state3, state1,
                  pl.BlockSpec((CONV_W, C), lambda b, c, t: (0, c)), vec,
                  gate_w, vec, gate_w, vec, vec],
        out_specs=[big, state1, state3],
        out_shape=[jax.ShapeDtypeStruct((B, T, Dr), BF16),
                   jax.ShapeDtypeStruct((B, 1, Dr), F32),
                   jax.ShapeDtypeStruct((B, hist, Dr), F32)],
        scratch_shapes=scratch,
        compiler_params=_cparams(("parallel", "parallel", "arbitrary")),
        name="rglru",
    )(xb, zg, conv0, h0.reshape(B, 1, Dr), conv_w, conv_b.reshape(1, Dr), w_r, b_r.reshape(1, Dr),
      w_i, b_i.reshape(1, Dr), lam.reshape(1, Dr))


def _mpr_kernel(a_ref, w_ref, x_ref, gate_ref, g_ref, o_ref, out_ref, *, tn):
    a = a_ref[...]
    for c in range(out_ref.shape[0]):
        out_ref[c] = _dot(a, w_ref[:, c * tn:(c + 1) * tn])
    out = jnp.concatenate([out_ref[c] for c in range(out_ref.shape[0])], axis=1)
    n = out * lax.rsqrt(jnp.mean(out * out, axis=-1, keepdims=True) + EPS) * g_ref[...]
    o_ref[...] = x_ref[...] + gate_ref[...] * n.reshape(o_ref.shape)


def _mpr_call(a, w, x, gate, g, *, bb, tt, tn, name):
    B, T, D = x.shape
    K = a.shape[1]
    nt = T // tt
    tm = bb * tt
    xspec = pl.BlockSpec((bb, tt, D), lambda i: (i // nt, i % nt, 0))
    return pl.pallas_call(
        functools.partial(_mpr_kernel, tn=tn),
        grid=((B // bb) * nt,),
        in_specs=[
            pl.BlockSpec((tm, K), lambda i: (i, 0)),
            pl.BlockSpec((K, D), lambda i: (0, 0), pipeline_mode=pl.Buffered(1)),
            xspec,
            pl.BlockSpec((bb, 1, D), lambda i: (i // nt, 0, 0)),
            pl.BlockSpec((1, D), lambda i: (0, 0)),
        ],
        out_specs=xspec,
        out_shape=jax.ShapeDtypeStruct((B, T, D), F32),
        scratch_shapes=[pltpu.VMEM((D // tn, tm, tn), F32)],
        compiler_params=_cparams(("parallel",)),
        name=name,
    )(a, w, x, gate, g.reshape(1, D))


def _rev_tri(n):
    j = lax.broadcasted_iota(jnp.int32, (n, n), 0)
    s = lax.broadcasted_iota(jnp.int32, (n, n), 1)
    return jnp.where(j >= s, 1.0, 0.0).astype(BF16)


def _sb_tile(q, kb, vb, tri, carry, mask):
    z2 = lax.dot_general(q, kb, (((1,), (1,)), ((), ())), preferred_element_type=F32)
    sp = _softplus2(z2)
    if mask is not None:
        sp = jnp.where(mask, sp, 0.0)
    hi = sp.astype(BF16)
    lo = (sp - hi.astype(F32)).astype(BF16)
    csum = _dot(hi, tri) + _dot(lo, tri)
    p = jnp.exp2(z2 - csum - carry)
    if mask is not None:
        p = jnp.where(mask, p, 0.0)
    return _dot(p.astype(BF16), vb), csum[:, 0:1]


def _attn_self_kernel(q_ref, zg_ref, k_ref, v_ref, o_ref, acc_ref, acct_ref, carry_ref,
                      z_ref, hi_ref, lo_ref, *, tq, tk, nh):
    i = pl.program_id(2)
    q_all = (q_ref[0] * QSCALE).astype(BF16)
    qs = [q_all[:, g * HEAD_DIM:(g + 1) * HEAD_DIM] for g in range(nh)]
    tri = _rev_tri(tk)
    nd = tq // tk
    n = i * nd

    def keys(ref, g, kt):
        ks = pl.multiple_of(kt * tk, tk)
        return ref[0, pl.ds(ks, tk), g * HEAD_DIM:(g + 1) * HEAD_DIM]

    def tile(g, kt, r0, mask):
        pv, tot = _sb_tile(qs[g][r0:], keys(k_ref, g, kt), keys(v_ref, g, kt), tri,
                           carry_ref[g, r0:, :], mask)
        acc_ref[g, r0:, :] += pv
        carry_ref[g, r0:, :] += tot

    def scores(g, kt, slot):
        z2 = lax.dot_general(qs[g], keys(k_ref, g, kt), (((1,), (1,)), ((), ())),
                             preferred_element_type=F32)
        sp = _softplus2(z2)
        hi = sp.astype(BF16)
        z_ref[g, slot] = z2
        hi_ref[g, slot] = hi
        lo_ref[g, slot] = (sp - hi.astype(F32)).astype(BF16)

    def weights(g, kt, slot):
        csum = _dot(hi_ref[g, slot], tri) + _dot(lo_ref[g, slot], tri)
        p = jnp.exp2(z_ref[g, slot] - csum - carry_ref[g])
        acct_ref[g] += lax.dot_general(keys(v_ref, g, kt).T, p.astype(BF16), (((1,), (1,)), ((), ())),
                                       preferred_element_type=F32)
        carry_ref[g] += csum[:, 0:1]

    acc_ref[...] = jnp.zeros_like(acc_ref)
    acct_ref[...] = jnp.zeros_like(acct_ref)
    carry_ref[...] = jnp.zeros_like(carry_ref)
    for g in range(nh):
        scores(g, jnp.maximum(n - 1, 0), 0)
    for c in reversed(range(nd)):
        rows = tq - c * tk
        row = lax.broadcasted_iota(jnp.int32, (rows, tk), 0)
        col = lax.broadcasted_iota(jnp.int32, (rows, tk), 1)
        for g in range(nh):
            tile(g, n + c, c * tk, col < row)

    def body(m, c):
        t = 2 * m
        for g in range(nh):
            scores(g, n - 2 - t, 1)
            weights(g, n - 1 - t, 0)
        for g in range(nh):
            scores(g, jnp.maximum(n - 3 - t, 0), 0)
            weights(g, n - 2 - t, 1)
        return c

    lax.fori_loop(0, n // 2, body, 0)
    acc = jnp.concatenate([acc_ref[g] + acct_ref[g].T for g in range(nh)], axis=1)
    o_ref[0] = (acc * _silu(zg_ref[0])).astype(o_ref.dtype)


def _attn_self_call(q, zg, k, v, *, tq, tk, nh):
    B, T, _ = q.shape
    assert (tq // tk) % 2 == 0 and tq % tk == 0 and T % tq == 0
    wide = nh * HEAD_DIM
    qspec = pl.BlockSpec((1, tq, wide), lambda b, h, i: (b, i, h))
    kspec = pl.BlockSpec((1, T, wide), lambda b, h, i: (b, 0, h))
    return pl.pallas_call(
        functools.partial(_attn_self_kernel, tq=tq, tk=tk, nh=nh),
        grid=(B, N_HEADS // nh, T // tq),
        in_specs=[qspec, qspec, kspec, kspec],
        out_specs=qspec,
        out_shape=jax.ShapeDtypeStruct((B, T, D_ATT), BF16),
        scratch_shapes=[pltpu.VMEM((nh, tq, HEAD_DIM), F32), pltpu.VMEM((nh, HEAD_DIM, tq), F32),
                        pltpu.VMEM((nh, tq, 1), F32),
                        pltpu.VMEM((nh, 2, tq, tk), F32), pltpu.VMEM((nh, 2, tq, tk), BF16),
                        pltpu.VMEM((nh, 2, tq, tk), BF16)],
        compiler_params=_cparams(("parallel", "parallel", "arbitrary")),
        name="sb_attn_self",
    )(q, zg, k, v)


def _attn_past_kernel(q_ref, zg_ref, kn_ref, vn_ref, kc_ref, vc_ref, o_ref, acc_ref, carry_ref, *, T, tp, blk):
    j = pl.program_id(2)
    nh = SUBLANES
    nc = tp // blk
    q_all = (q_ref[0] * QSCALE).astype(BF16)

    @pl.when(j == 0)
    def _():
        lane = lax.broadcasted_iota(jnp.int32, q_all.shape, 1)
        q_bd = jnp.concatenate(
            [jnp.where((lane >= h * HEAD_DIM) & (lane < (h + 1) * HEAD_DIM), q_all, jnp.zeros_like(q_all))
             for h in range(nh)], axis=0)
        t_idx = jnp.concatenate([lax.broadcasted_iota(jnp.int32, (T, T), 0)] * nh, axis=0)
        s_idx = lax.broadcasted_iota(jnp.int32, (nh * T, T), 1)
        pv, tot = _sb_tile(q_bd, kn_ref[0].astype(BF16), vn_ref[0].astype(BF16), _rev_tri(T),
                           jnp.zeros((nh * T, 1), F32), s_idx < t_idx)
        for h in range(nh):
            acc_ref[h] = pv[h * T:(h + 1) * T, h * HEAD_DIM:(h + 1) * HEAD_DIM]
            carry_ref[h] = tot[h * T:(h + 1) * T]

    kc2 = kc_ref.reshape(tp * nh, HEAD_DIM)
    vc2 = vc_ref.reshape(tp * nh, HEAD_DIM)
    zs, lks = [], []
    for h in range(nh):
        kh = kc2[pl.ds(h, tp, stride=nh), :].astype(BF16)
        z = lax.dot_general(q_all[:, h * HEAD_DIM:(h + 1) * HEAD_DIM], kh, (((1,), (1,)), ((), ())),
                            preferred_element_type=F32)
        zs.append(z)
        lks.append(_softplus2(z))
    stack = jnp.concatenate([lk[:, c * blk:(c + 1) * blk] for lk in lks for c in range(nc)], axis=0)
    hi = stack.astype(BF16)
    lo = (stack - hi.astype(F32)).astype(BF16)
    tri = _rev_tri(blk)
    cs = _dot(hi, tri) + _dot(lo, tri)
    for h in range(nh):
        carry = carry_ref[h]
        ps = [None] * nc
        for c in reversed(range(nc)):
            r0 = (h * nc + c) * T
            csum = cs[r0:r0 + T]
            ps[c] = jnp.exp2(zs[h][:, c * blk:(c + 1) * blk] - csum - carry)
            carry = carry + csum[:, 0:1]
        p = jnp.concatenate(ps, axis=1).astype(BF16)
        vh = vc2[pl.ds(h, tp, stride=nh), :].astype(BF16)
        acc_ref[h] += _dot(p, vh)
        carry_ref[h] = carry

    @pl.when(j == pl.num_programs(2) - 1)
    def _():
        acc = jnp.concatenate([acc_ref[h] for h in range(nh)], axis=1)
        o_ref[0] = (acc * _silu(zg_ref[0])).astype(o_ref.dtype)


def _attn_past_call(q, zg, k_new, v_new, k_past, v_past, *, tp, blk):
    B, T, _ = q.shape
    P = k_past.shape[1]
    nh = SUBLANES
    ng = N_HEADS // nh
    nt = P // tp
    wide = nh * HEAD_DIM
    qspec = pl.BlockSpec((1, T, wide), lambda b, g, j: (b, 0, g))
    pspec = pl.BlockSpec((None, tp, None, nh, HEAD_DIM), lambda b, g, j: (b, nt - 1 - j, g, 0, 0))
    k5 = k_past.reshape(B, P, ng, nh, HEAD_DIM)
    v5 = v_past.reshape(B, P, ng, nh, HEAD_DIM)
    return pl.pallas_call(
        functools.partial(_attn_past_kernel, T=T, tp=tp, blk=blk),
        grid=(B, ng, nt),
        in_specs=[qspec, qspec, qspec, qspec, pspec, pspec],
        out_specs=qspec,
        out_shape=jax.ShapeDtypeStruct((B, T, D_ATT), BF16),
        scratch_shapes=[pltpu.VMEM((nh, T, HEAD_DIM), F32), pltpu.VMEM((nh, T, 1), F32)],
        compiler_params=_cparams(("parallel", "parallel", "arbitrary")),
        name="sb_attn_past",
    )(q, zg, k_new, v_new, k5, v5)


class _Plan(NamedTuple):
    in_rows: tuple
    out_rows: tuple
    lru_rows: tuple
    cols: int
    lru_blocks: int
    tq: int
    heads: int
    tp: int


def _plan(B, T):
    long_seq = T >= 4 * MXU_DIM
    return _Plan(
        in_rows=(1, 4 * MXU_DIM) if long_seq else (B, T),
        out_rows=(1, 2 * MXU_DIM) if long_seq else (B, T),
        lru_rows=(1, 2 * MXU_DIM) if long_seq else (B, T),
        cols=(2 if long_seq else 4) * MXU_DIM, lru_blocks=4, tq=2 * MXU_DIM, heads=4, tp=8 * MXU_DIM)


def _run_group(x, mods, h0, conv0, past_k, past_v, p):
    B, T, D = x.shape
    plan = _plan(B, T)
    in_bb, in_tt = plan.in_rows
    out_bb, out_tt = plan.out_rows
    lru_bb, lru_tt = plan.lru_rows
    hs, convs = [], []
    k_new = v_new = None
    for l in range(DEPTH):
        shift, scale, gate = mods[l]
        if l < N_A:
            xb, zg = _nmm_call(x, p["g_pre"][l], scale, shift, p["w_in_a"][l],
                               bb=in_bb, tt=in_tt, tn=plan.cols, name="in_proj_a")
            yz, ht, cn = _lru_call(xb.reshape(B, T, D_RNN), zg.reshape(B, T, D_RNN), conv0[l], h0[l],
                                   p["conv_w"][l], p["conv_b"][l], p["w_rgate"][l], p["b_rgate"][l],
                                   p["w_igate"][l], p["b_igate"][l], p["lru_lambda"][l],
                                   bb=lru_bb, tt=lru_tt, nblk=plan.lru_blocks)
            hs.append(ht.reshape(B, D_RNN))
            convs.append(cn)
            x = _mpr_call(yz.reshape(B * T, D_RNN), p["w_out_a"][l], x, gate, p["g_post"][l],
                          bb=out_bb, tt=out_tt, tn=plan.cols, name="out_proj_a")
        else:
            if l == N_A:
                zero = jnp.zeros_like(scale)
                k_new, v_new, k16, v16 = _nmm_call(x, p["g_kv"], zero, zero, p["w_kv"], bb=in_bb,
                                                   tt=in_tt, tn=plan.cols, name="kv_proj",
                                                   bf16_copies=True)
            j = l - N_A
            q, zg = _nmm_call(x, p["g_pre"][l], scale, shift, p["w_in_b"][j],
                              bb=in_bb, tt=in_tt, tn=plan.cols, name="in_proj_b")
            q3, zg3 = q.reshape(B, T, D_ATT), zg.reshape(B, T, D_ATT)
            k3, v3 = k16.reshape(B, T, D_ATT), v16.reshape(B, T, D_ATT)
            if past_k is None:
                oz = _attn_self_call(q3, zg3, k3, v3, tq=plan.tq, tk=MXU_DIM, nh=plan.heads)
            else:
                oz = _attn_past_call(q3, zg3, k3, v3, past_k, past_v, tp=plan.tp, blk=MXU_DIM)
            x = _mpr_call(oz.reshape(B * T, D_ATT), p["w_out_b"][j], x, gate, p["g_post"][l],
                          bb=out_bb, tt=out_tt, tn=plan.cols, name="out_proj_b")
    return (x, k_new.reshape(B, T, N_HEADS, HEAD_DIM), v_new.reshape(B, T, N_HEADS, HEAD_DIM),
            jnp.stack(hs), jnp.stack(convs))


def kernel(x_prompt, x_sample, c_prompt, c_sample, cache_k, cache_v, state_lru, state_conv, g_pre, g_post, w_ada, b_ada, w_in_a, conv_w, conv_b, w_rgate, b_rgate, w_igate, b_igate, lru_lambda, w_out_a, g_kv, w_kv, w_in_b, w_out_b):
    Bp, Bs = x_prompt.shape[0], x_sample.shape[0]
    p = dict(g_pre=g_pre, g_post=g_post, conv_w=conv_w, conv_b=conv_b, b_rgate=b_rgate, b_igate=b_igate,
             lru_lambda=lru_lambda, g_kv=g_kv,
             w_in_a=w_in_a.astype(BF16), w_rgate=w_rgate.astype(BF16), w_igate=w_igate.astype(BF16),
             w_out_a=w_out_a.astype(BF16), w_kv=w_kv.astype(BF16), w_in_b=w_in_b.astype(BF16),
             w_out_b=w_out_b.astype(BF16))

    rows = Bp + Bs
    rows_pad = -(-rows // SUBLANES) * SUBLANES
    c_all = jnp.concatenate([c_prompt, c_sample, jnp.zeros((rows_pad - rows, D_MODEL), F32)], axis=0)
    m = _ada_call(c_all, w_ada, b_ada)

    def mods_of(lo, hi):
        out = []
        for l in range(DEPTH):
            ml = m[l, lo:hi]
            out.append(tuple(ml[:, i * D_MODEL:(i + 1) * D_MODEL][:, None, :] for i in range(3)))
        return out

    h0_p = jnp.zeros((N_A, Bp, D_RNN), F32)
    conv0_p = jnp.zeros((N_A, Bp, CONV_W - 1, D_RNN), F32)
    y_p, k_p, v_p, lru_p, conv_p = _run_group(x_prompt, mods_of(0, Bp), h0_p, conv0_p, None, None, p)
    y_s, k_s, v_s, lru_s, conv_s = _run_group(x_sample, mods_of(Bp, rows), state_lru, state_conv,
                                              cache_k, cache_v, p)
    return (y_p, y_s, k_p, v_p, k_s, v_s, lru_p, lru_s, conv_p, conv_s)
```
